```python
import math
import jax, jax.numpy as jnp
from jax import lax
import numpy as np

D_MODEL = 1024
BATCH = 2
SEQ = 8192
DEPTH = 2

N_A_LAYERS = (DEPTH + 1) // 2
N_B_LAYERS = DEPTH // 2
DIFF_HEAD_DIM = 64
DIFF_V_DIM = 2 * DIFF_HEAD_DIM
DIFF_HEADS = D_MODEL // DIFF_V_DIM
MOBA_HEAD_DIM = 64
MOBA_HEADS = D_MODEL // MOBA_HEAD_DIM
MOBA_BLOCK = 256
MOBA_TOPK = 3
D_FF = 4 * D_MODEL
REL_BUCKETS = 32
REL_MAX_DIST = 128
REL_HEADS = 2 * DIFF_HEADS
Q_BLOCK = 128
DEEPNORM_ALPHA = (2 * DEPTH) ** 0.25
DEEPNORM_BETA = (8 * DEPTH) ** -0.25
LN_EPS = 1e-5

kernel_name = 'yoco_diffattn_moba_deepnorm'


def rel_bucket(rel):
    n = jnp.maximum(rel, 0)
    max_exact = REL_BUCKETS // 2
    nf = jnp.maximum(n, max_exact).astype(jnp.float32)
    large = max_exact + (jnp.log(nf / max_exact) / math.log(REL_MAX_DIST / max_exact)
                         * (REL_BUCKETS - max_exact)).astype(jnp.int32)
    large = jnp.minimum(large, REL_BUCKETS - 1)
    return jnp.where(n < max_exact, n, large)


def layer_norm(x, g, b):
    xf = x.astype(jnp.float32)
    mu = jnp.mean(xf, axis=-1, keepdims=True)
    var = jnp.mean(jnp.square(xf - mu), axis=-1, keepdims=True)
    y = (xf - mu) * lax.rsqrt(var + LN_EPS) * g.astype(jnp.float32) + b.astype(jnp.float32)
    return y.astype(x.dtype)


def diff_lambda_init(layer):
    return 0.8 - 0.6 * math.exp(-0.3 * layer)


def squared_relu_mlp(x, w_up, w_down):
    return jnp.square(jax.nn.relu(x @ w_up)) @ w_down


def diff_attention(x, w_in, lam_p, subln_g, w_out, rel_bias, lam_init):
    B, S, _ = x.shape
    H, dh, dv = DIFF_HEADS, DIFF_HEAD_DIM, DIFF_V_DIM
    qkv = x @ w_in
    q, k, v = jnp.split(qkv, [2 * H * dh, 4 * H * dh], axis=-1)
    q = q.reshape(B, S, H, 2, dh) * (dh ** -0.5)
    k = k.reshape(B, S, H, 2, dh)
    v = v.reshape(B, S, H, dv)
    lf = lam_p.astype(jnp.float32)
    lam = jnp.exp(jnp.sum(lf[0] * lf[1])) - jnp.exp(jnp.sum(lf[2] * lf[3])) + lam_init
    kpos = jnp.arange(S)

    def block(i):
        q_blk = lax.dynamic_slice_in_dim(q, i * Q_BLOCK, Q_BLOCK, axis=1)
        qpos = i * Q_BLOCK + jnp.arange(Q_BLOCK)
        rel = qpos[:, None] - kpos[None, :]
        bias = rel_bias[rel_bucket(rel)].astype(jnp.float32)
        bias = bias.reshape(Q_BLOCK, S, H, 2).transpose(2, 3, 0, 1)
        s = jnp.einsum('bqhmd,bkhmd->bhmqk', q_blk, k).astype(jnp.float32) + bias
        s = jnp.where(rel >= 0, s, -jnp.inf)
        p = jax.nn.softmax(s, axis=-1)
        a = p[:, :, 0] - lam * p[:, :, 1]
        return jnp.einsum('bhqk,bkhe->bqhe', a.astype(v.dtype), v)

    o = lax.map(block, jnp.arange(S // Q_BLOCK))
    o = o.transpose(1, 0, 2, 3, 4).reshape(B, S, H, dv).astype(jnp.float32)
    o = o * lax.rsqrt(jnp.mean(o * o, axis=-1, keepdims=True) + LN_EPS)
    o = o * subln_g.astype(jnp.float32) * (1.0 - lam_init)
    return o.reshape(B, S, H * dv).astype(x.dtype) @ w_out


def shared_kv(x, w_kv):
    B, S, _ = x.shape
    H, dh, L = MOBA_HEADS, MOBA_HEAD_DIM, MOBA_BLOCK
    k, v = jnp.split(x @ w_kv, 2, axis=-1)
    k = k.reshape(B, S, H, dh).transpose(0, 2, 1, 3)
    v = v.reshape(B, S, H, dh).transpose(0, 2, 1, 3)
    n_blk = -(-S // L)
    pad = n_blk * L - S
    k = jnp.pad(k, ((0, 0), (0, 0), (0, pad), (0, 0)))
    v = jnp.pad(v, ((0, 0), (0, 0), (0, pad), (0, 0)))
    k_blocks = k.reshape(B, H, n_blk, L, dh)
    v_blocks = v.reshape(B, H, n_blk, L, dh)
    k_mean = jnp.mean(k_blocks.astype(jnp.float32), axis=3)
    return k_blocks, v_blocks, k_mean


def moba_attention(x, w_q, w_out, k_blocks, v_blocks, k_mean, rel_bias):
    B, S, _ = x.shape
    H, dh, L = MOBA_HEADS, MOBA_HEAD_DIM, MOBA_BLOCK
    n_blk = k_blocks.shape[2]
    topk = min(MOBA_TOPK, n_blk)
    scale = dh ** -0.5
    q = (x @ w_q).reshape(B, S, H, dh).transpose(0, 2, 1, 3)
    bi = jnp.arange(B)[:, None, None]
    hi = jnp.arange(H)[None, :, None]
    blk_ids = jnp.arange(n_blk)
    offs = jnp.arange(L)

    def block(i):
        q_c = lax.dynamic_slice_in_dim(q, i * Q_BLOCK, Q_BLOCK, axis=2)
        qpos = i * Q_BLOCK + jnp.arange(Q_BLOCK)
        own = (i * Q_BLOCK) // L
        g = jnp.einsum('bhqd,bhnd->bhqn', q_c.astype(jnp.float32), k_mean)
        past = blk_ids[None, :] < (qpos // L)[:, None]
        g = jnp.where(past, g, -jnp.inf)
        g_val, sel = lax.top_k(g, topk)
        sel_ok = jnp.isfinite(g_val)
        k_own = lax.dynamic_index_in_dim(k_blocks, own, axis=2, keepdims=False)
        v_own = lax.dynamic_index_in_dim(v_blocks, own, axis=2, keepdims=False)
        rel_own = qpos[:, None] - (own * L + offs)[None, :]
        b_own = rel_bias[rel_bucket(rel_own)].astype(jnp.float32).transpose(2, 0, 1)
        s_own = jnp.einsum('bhqd,bhkd->bhqk', q_c, k_own).astype(jnp.float32) * scale + b_own
        logits = [jnp.where(rel_own >= 0, s_own, -jnp.inf)]
        for j in range(topk):
            idx = sel[..., j]
            k_sel = k_blocks[bi, hi, idx]
            rel = qpos[:, None] - (idx[..., None] * L + offs)
            b_sel = rel_bias[rel_bucket(rel), hi[..., None]].astype(jnp.float32)
            s = jnp.einsum('bhqd,bhqkd->bhqk', q_c, k_sel).astype(jnp.float32) * scale + b_sel
            logits.append(jnp.where(sel_ok[..., j, None], s, -jnp.inf))
        p = jax.nn.softmax(jnp.concatenate(logits, axis=-1), axis=-1).astype(v_blocks.dtype)
        o = jnp.einsum('bhqk,bhkd->bhqd', p[..., :L], v_own)
        for j in range(topk):
            v_sel = v_blocks[bi, hi, sel[..., j]]
            o = o + jnp.einsum('bhqk,bhqkd->bhqd', p[..., (j + 1) * L:(j + 2) * L], v_sel)
        return o

    o = lax.map(block, jnp.arange(S // Q_BLOCK))
    o = o.transpose(1, 0, 3, 2, 4).reshape(B, S, H * dh)
    return o @ w_out


def setup_inputs(seed: int = 0) -> dict:
    key = jax.random.key(seed)
    ks = jax.random.split(key, 20)
    f32 = jnp.float32
    D = D_MODEL
    s_in = D ** -0.5
    w_qk_a = jax.random.normal(ks[1], (N_A_LAYERS, D, 4 * D_MODEL // 2), f32) * s_in
    w_v_a = jax.random.normal(ks[2], (N_A_LAYERS, D, DIFF_HEADS * DIFF_V_DIM), f32) * s_in * DEEPNORM_BETA
    w_k_sh = jax.random.normal(ks[5], (D, MOBA_HEADS * MOBA_HEAD_DIM), f32) * s_in
    w_v_sh = jax.random.normal(ks[6], (D, MOBA_HEADS * MOBA_HEAD_DIM), f32) * s_in * DEEPNORM_BETA
    return {
        'x': jax.random.normal(ks[0], (BATCH, SEQ, D), f32),
        'w_in_a': jnp.concatenate([w_qk_a, w_v_a], axis=-1),
        'lam_a': jax.random.normal(ks[3], (N_A_LAYERS, 4, DIFF_HEAD_DIM), f32) * 0.1,
        'subln_a': 1.0 + 0.02 * jax.random.normal(ks[4], (N_A_LAYERS, DIFF_V_DIM), f32),
        'w_out_a': jax.random.normal(ks[7], (N_A_LAYERS, D, D), f32) * s_in * DEEPNORM_BETA,
        'w_kv_shared': jnp.concatenate([w_k_sh, w_v_sh], axis=-1),
        'w_q_b': jax.random.normal(ks[8], (N_B_LAYERS, D, MOBA_HEADS * MOBA_HEAD_DIM), f32) * s_in,
        'w_out_b': jax.random.normal(ks[9], (N_B_LAYERS, D, D), f32) * s_in * DEEPNORM_BETA,
        'rel_bias': jax.random.normal(ks[10], (REL_BUCKETS, REL_HEADS), f32) * 0.5,
        'ln1_g': 1.0 + 0.02 * jax.random.normal(ks[11], (DEPTH, D), f32),
        'ln1_b': 0.02 * jax.random.normal(ks[12], (DEPTH, D), f32),
        'ln2_g': 1.0 + 0.02 * jax.random.normal(ks[13], (DEPTH, D), f32),
        'ln2_b': 0.02 * jax.random.normal(ks[14], (DEPTH, D), f32),
        'w_up': jax.random.normal(ks[15], (DEPTH, D, D_FF), f32) * s_in,
        'w_down': jax.random.normal(ks[16], (DEPTH, D_FF, D), f32) * (D_FF ** -0.5) * DEEPNORM_BETA,
    }


def reference(x, w_in_a, lam_a, subln_a, w_out_a, w_kv_shared, w_q_b, w_out_b, rel_bias,
              ln1_g, ln1_b, ln2_g, ln2_b, w_up, w_down):
    h = x
    kv = None
    for layer in range(DEPTH):
        if layer < N_A_LAYERS:
            mix = diff_attention(h, w_in_a[layer], lam_a[layer], subln_a[layer], w_out_a[layer],
                                 rel_bias, diff_lambda_init(layer))
        else:
            if kv is None:
                kv = shared_kv(h, w_kv_shared)
            j = layer - N_A_LAYERS
            mix = moba_attention(h, w_q_b[j], w_out_b[j], kv[0], kv[1], kv[2], rel_bias)
        h = layer_norm(DEEPNORM_ALPHA * h + mix, ln1_g[layer], ln1_b[layer])
        h = layer_norm(DEEPNORM_ALPHA * h + squared_relu_mlp(h, w_up[layer], w_down[layer]),
                       ln2_g[layer], ln2_b[layer])
    return h
```

```python
import functools
import math

import jax
import jax.numpy as jnp
from jax import lax
from jax.experimental import pallas as pl
from jax.experimental.pallas import tpu as pltpu

F32 = jnp.float32
BF16 = jnp.bfloat16

D_MODEL = 1024
D_FF = 4 * D_MODEL
DEPTH = 2
HEAD_DIM = 64
PAIR = 2 * HEAD_DIM
N_PAIRS = D_MODEL // PAIR
T = 256
MOBA_TOPK = 3
REL_BUCKETS = 32
REL_MAX_DIST = 128
LN_EPS = 1e-5
ALPHA = (2 * DEPTH) ** 0.25
LAM_INIT_A = 0.8 - 0.6 * math.exp(-0.3 * 0)
NEG = -1e30
V7X_VMEM_LIMIT = 56 * 1024 * 1024
TM = 512


def _params(n_axes):
    return pltpu.CompilerParams(
        dimension_semantics=("arbitrary",) * n_axes,
        vmem_limit_bytes=V7X_VMEM_LIMIT,
    )


def _nt_dot(a, b):
    return lax.dot_general(a, b, (((1,), (1,)), ((), ())), preferred_element_type=F32)


def _layer_norm(y, g, b):
    mu = jnp.mean(y, axis=-1, keepdims=True)
    yc = y - mu
    var = jnp.mean(yc * yc, axis=-1, keepdims=True)
    return yc * lax.rsqrt(var + LN_EPS) * g + b


def _bias_tile_kernel(tab_ref, out_ref):
    pr = pl.program_id(0)
    j = lax.broadcasted_iota(jnp.int32, (T, T), 0)
    i = lax.broadcasted_iota(jnp.int32, (T, T), 1)
    max_exact = REL_BUCKETS // 2
    for o in range(2):
        d = i - j + o * T
        n = jnp.maximum(d, 0)
        nf = jnp.maximum(n, max_exact).astype(F32)
        large = max_exact + (jnp.log(nf / max_exact) / math.log(REL_MAX_DIST / max_exact)
                             * (REL_BUCKETS - max_exact)).astype(jnp.int32)
        large = jnp.minimum(large, REL_BUCKETS - 1)
        bucket = jnp.where(n < max_exact, n, large)
        for m in range(2):
            col = 2 * pr + m
            far = tab_ref[REL_BUCKETS - 1, col]
            val = jnp.zeros((T, T), F32)
            for b in range(REL_BUCKETS - 1):
                val = jnp.where(bucket == b, tab_ref[b, col] - far, val)
            if o == 0:
                val = jnp.where(d >= 0, val, NEG)
            out_ref[0, o, :, m * T:(m + 1) * T] = val


def _bias_tiles(rel_bias):
    return pl.pallas_call(
        _bias_tile_kernel,
        grid=(N_PAIRS,),
        in_specs=[pl.BlockSpec(memory_space=pltpu.SMEM)],
        out_specs=pl.BlockSpec((1, 2, T, 2 * T), lambda p: (p, 0, 0, 0)),
        out_shape=jax.ShapeDtypeStruct((N_PAIRS, 2, T, 2 * T), F32),
        compiler_params=_params(1),
        name="bias_tiles",
    )(rel_bias)


def _store_t_blocks(ref, xt):
    for t in range(TM // T):
        for p in range(N_PAIRS):
            ref[0, p, t] = xt[p * PAIR:(p + 1) * PAIR, t * T:(t + 1) * T].astype(BF16)


def _store_k_blocks(ref, k):
    for t in range(TM // T):
        for p in range(N_PAIRS):
            ref[0, p, t] = k[t * T:(t + 1) * T, p * PAIR:(p + 1) * PAIR].astype(BF16)


def _proj_a_kernel(x_ref, wqt_ref, wk_ref, wvt_ref, qt_ref, k_ref, vt_ref):
    xb = x_ref[0].astype(BF16)
    _store_t_blocks(qt_ref, _nt_dot(wqt_ref[...], xb) * (HEAD_DIM ** -0.5))
    _store_k_blocks(k_ref, jnp.dot(xb, wk_ref[...], preferred_element_type=F32))
    _store_t_blocks(vt_ref, _nt_dot(wvt_ref[...], xb))


def _blocked_t_spec():
    return pl.BlockSpec((1, N_PAIRS, TM // T, PAIR, T), lambda b, i: (b, 0, i, 0, 0))


def _blocked_k_spec():
    return pl.BlockSpec((1, N_PAIRS, TM // T, T, PAIR), lambda b, i: (b, 0, i, 0, 0))


def _weight_spec(shape):
    return pl.BlockSpec(shape, lambda *_: (0,) * len(shape))


def _proj_a(x, wqt, wk, wvt):
    bsz, seq, _ = x.shape
    nb = seq // T
    t_shape = jax.ShapeDtypeStruct((bsz, N_PAIRS, nb, PAIR, T), BF16)
    k_shape = jax.ShapeDtypeStruct((bsz, N_PAIRS, nb, T, PAIR), BF16)
    return pl.pallas_call(
        _proj_a_kernel,
        grid=(bsz, seq // TM),
        in_specs=[
            pl.BlockSpec((1, TM, D_MODEL), lambda b, i: (b, i, 0)),
            _weight_spec((D_MODEL, D_MODEL)),
            _weight_spec((D_MODEL, D_MODEL)),
            _weight_spec((D_MODEL, D_MODEL)),
        ],
        out_specs=[_blocked_t_spec(), _blocked_k_spec(), _blocked_t_spec()],
        out_shape=[t_shape, k_shape, t_shape],
        compiler_params=_params(2),
        name="proj_a",
    )(x, wqt, wk, wvt)


def _proj_kv_kernel(h_ref, wk_ref, wvt_ref, k_ref, vt_ref, kmean_ref):
    hb = h_ref[0].astype(BF16)
    k = jnp.dot(hb, wk_ref[...], preferred_element_type=F32)
    _store_k_blocks(k_ref, k)
    for t in range(TM // T):
        blk = pl.program_id(1) * (TM // T) + t
        kmean_ref[0, pl.ds(blk, 1), :] = jnp.mean(k[t * T:(t + 1) * T], axis=0, keepdims=True)
    _store_t_blocks(vt_ref, _nt_dot(wvt_ref[...], hb))


def _proj_kv(h, wk, wvt):
    bsz, seq, _ = h.shape
    nb = seq // T
    return pl.pallas_call(
        _proj_kv_kernel,
        grid=(bsz, seq // TM),
        in_specs=[
            pl.BlockSpec((1, TM, D_MODEL), lambda b, i: (b, i, 0)),
            _weight_spec((D_MODEL, D_MODEL)),
            _weight_spec((D_MODEL, D_MODEL)),
        ],
        out_specs=[
            _blocked_k_spec(),
            _blocked_t_spec(),
            pl.BlockSpec((1, nb, D_MODEL), lambda b, i: (b, 0, 0)),
        ],
        out_shape=[
            jax.ShapeDtypeStruct((bsz, N_PAIRS, nb, T, PAIR), BF16),
            jax.ShapeDtypeStruct((bsz, N_PAIRS, nb, PAIR, T), BF16),
            jax.ShapeDtypeStruct((bsz, nb, D_MODEL), F32),
        ],
        compiler_params=_params(2),
        name="proj_kv",
    )(h, wk, wvt)


def _proj_q_gate_kernel(h_ref, wqt_ref, kmean_ref, qt_ref, mask_ref, *, n_blocks):
    hb = h_ref[0].astype(BF16)
    qt = _nt_dot(wqt_ref[...], hb)
    _store_t_blocks(qt_ref, qt * (HEAD_DIM ** -0.5))

    km = kmean_ref[0]
    lane = lax.broadcasted_iota(jnp.int32, (n_blocks, PAIR), 1)
    blk = lax.broadcasted_iota(jnp.int32, (n_blocks, TM), 0)
    col = lax.broadcasted_iota(jnp.int32, (1, TM), 1)
    q_blk = pl.program_id(1) * (TM // T) + lax.shift_right_logical(col, T.bit_length() - 1)
    past = blk < q_blk
    for p in range(N_PAIRS):
        km_p = km[:, p * PAIR:(p + 1) * PAIR]
        km2 = jnp.concatenate(
            [jnp.where(lane < HEAD_DIM, km_p, 0.0), jnp.where(lane >= HEAD_DIM, km_p, 0.0)],
            axis=0)
        g2 = jnp.dot(km2, qt[p * PAIR:(p + 1) * PAIR], preferred_element_type=F32,
                     precision=lax.Precision.HIGHEST)
        for sub in range(2):
            g = g2[sub * n_blocks:(sub + 1) * n_blocks]
            cnt = jnp.zeros((n_blocks, TM), F32)
            for m in range(n_blocks):
                gm = g[m:m + 1]
                beats = (gm > g) | ((gm == g) & (blk > m))
                cnt = cnt + jnp.where(beats & (q_blk > m), 1.0, 0.0)
            add = jnp.where(past & (cnt < MOBA_TOPK), 0.0, NEG)
            for t in range(TM // T):
                mask_ref[0, p, t, :, sub * T:(sub + 1) * T] = add[:, t * T:(t + 1) * T]


def _proj_q_gate(h, wqt, kmean):
    bsz, seq, _ = h.shape
    nb = seq // T
    return pl.pallas_call(
        functools.partial(_proj_q_gate_kernel, n_blocks=nb),
        grid=(bsz, seq // TM),
        in_specs=[
            pl.BlockSpec((1, TM, D_MODEL), lambda b, i: (b, i, 0)),
            _weight_spec((D_MODEL, D_MODEL)),
            pl.BlockSpec((1, nb, D_MODEL), lambda b, i: (b, 0, 0)),
        ],
        out_specs=[
            _blocked_t_spec(),
            pl.BlockSpec((1, N_PAIRS, TM // T, nb, 2 * T), lambda b, i: (b, 0, i, 0, 0)),
        ],
        out_shape=[
            jax.ShapeDtypeStruct((bsz, N_PAIRS, nb, PAIR, T), BF16),
            jax.ShapeDtypeStruct((bsz, N_PAIRS, nb, nb, 2 * T), F32),
        ],
        compiler_params=_params(2),
        name="proj_q_gate",
    )(h, wqt, kmean)


def _fill_q2(q2_ref, qt):
    zeros = jnp.zeros((HEAD_DIM, T), BF16)
    q2_ref[0:HEAD_DIM, 0:T] = qt[0:HEAD_DIM]
    q2_ref[0:HEAD_DIM, T:2 * T] = zeros
    q2_ref[HEAD_DIM:PAIR, 0:T] = zeros
    q2_ref[HEAD_DIM:PAIR, T:2 * T] = qt[HEAD_DIM:PAIR]


def _softmax_step(s, m_ref, l_ref, first):
    m_blk = jnp.max(s, axis=0, keepdims=True)
    if first:
        m_new = m_blk
        alpha = None
    else:
        m_old = m_ref[...]
        m_new = jnp.maximum(m_old, m_blk)
        alpha = jnp.exp(m_old - m_new)
    p = jnp.exp(s - m_new)
    p_sum = jnp.sum(p, axis=0, keepdims=True)
    l_ref[...] = p_sum if first else alpha * l_ref[...] + p_sum
    m_ref[...] = m_new
    return p.astype(BF16), alpha


def _diff_attn_kernel(qt_ref, k_ref, vt_ref, bias_ref, lam_ref, subln_ref, o_ref,
                      q2_ref, m_ref, l_ref, acc_ref):
    i = pl.program_id(2)
    _fill_q2(q2_ref, qt_ref[0, 0, 0])

    def step(j, extra, first=False):
        s = jnp.dot(k_ref[0, 0, j], q2_ref[...], preferred_element_type=F32)
        if extra is not None:
            s = s + extra
        p, alpha = _softmax_step(s, m_ref, l_ref, first)
        pv = jnp.dot(vt_ref[0, 0, j], p, preferred_element_type=F32)
        acc_ref[...] = pv if first else alpha * acc_ref[...] + pv

    step(i, bias_ref[0, 0], first=True)

    @pl.when(i > 0)
    def _():
        step(i - 1, bias_ref[0, 1])

    def body(j, carry):
        step(j, None)
        return carry

    lax.fori_loop(0, jnp.maximum(i - 1, 0), body, 0)

    lp = lam_ref[...]
    lam = (jnp.exp(jnp.sum(lp[0:1] * lp[1:2], axis=1, keepdims=True))
           - jnp.exp(jnp.sum(lp[2:3] * lp[3:4], axis=1, keepdims=True)) + LAM_INIT_A)
    o_all = acc_ref[...] / l_ref[...]
    o = o_all[:, 0:T] - lam * o_all[:, T:2 * T]
    o = o * lax.rsqrt(jnp.mean(o * o, axis=0, keepdims=True) + LN_EPS)
    o = o * subln_ref[...] * (1.0 - LAM_INIT_A)
    o_ref[0] = o.T.astype(BF16)


def _attn_in_specs(nb):
    return [
        pl.BlockSpec((1, 1, 1, PAIR, T), lambda b, p, i: (b, p, i, 0, 0)),
        pl.BlockSpec((1, 1, nb, T, PAIR), lambda b, p, i: (b, p, 0, 0, 0)),
        pl.BlockSpec((1, 1, nb, PAIR, T), lambda b, p, i: (b, p, 0, 0, 0)),
        pl.BlockSpec((1, 2, T, 2 * T), lambda b, p, i: (p, 0, 0, 0)),
    ]


def _attn_scratch():
    return [
        pltpu.VMEM((PAIR, 2 * T), BF16),
        pltpu.VMEM((1, 2 * T), F32),
        pltpu.VMEM((1, 2 * T), F32),
        pltpu.VMEM((PAIR, 2 * T), F32),
    ]


def _diff_attn(qt, k, vt, bias, lam_p, subln_col):
    bsz, _, nb, _, _ = qt.shape
    return pl.pallas_call(
        _diff_attn_kernel,
        grid=(bsz, N_PAIRS, nb),
        in_specs=_attn_in_specs(nb) + [
            pl.BlockSpec((4, HEAD_DIM), lambda b, p, i: (0, 0)),
            pl.BlockSpec((PAIR, 1), lambda b, p, i: (0, 0)),
        ],
        out_specs=pl.BlockSpec((1, T, PAIR), lambda b, p, i: (b, i, p)),
        out_shape=jax.ShapeDtypeStruct((bsz, nb * T, D_MODEL), BF16),
        scratch_shapes=_attn_scratch(),
        compiler_params=_params(3),
        name="diff_attn",
    )(qt, k, vt, bias, lam_p, subln_col)


def _moba_attn_kernel(qt_ref, k_ref, vt_ref, bias_ref, mask_ref, o_ref,
                      q2_ref, m_ref, l_ref, acc_ref):
    i = pl.program_id(2)
    _fill_q2(q2_ref, qt_ref[0, 0, 0])

    def step(j, extra, first=False):
        s = jnp.dot(k_ref[0, 0, j], q2_ref[...], preferred_element_type=F32)
        s = s + extra
        p, alpha = _softmax_step(s, m_ref, l_ref, first)
        vt = vt_ref[0, 0, j]
        pv = jnp.concatenate(
            [jnp.dot(vt[0:HEAD_DIM], p[:, 0:T], preferred_element_type=F32),
             jnp.dot(vt[HEAD_DIM:PAIR], p[:, T:2 * T], preferred_element_type=F32)],
            axis=1)
        acc_ref[...] = pv if first else alpha * acc_ref[...] + pv

    step(i, bias_ref[0, 0], first=True)

    @pl.when(i > 0)
    def _():
        step(i - 1, bias_ref[0, 1] + mask_ref[0, 0, 0, pl.ds(i - 1, 1), :])

    def body(j, carry):
        step(j, mask_ref[0, 0, 0, pl.ds(j, 1), :])
        return carry

    lax.fori_loop(0, jnp.maximum(i - 1, 0), body, 0)

    o_all = acc_ref[...] / l_ref[...]
    o = jnp.concatenate([o_all[:, 0:T], o_all[:, T:2 * T]], axis=0)
    o_ref[0] = o.T.astype(BF16)


def _moba_attn(qt, k, vt, bias, mask):
    bsz, _, nb, _, _ = qt.shape
    scratch = _attn_scratch()
    scratch[3] = pltpu.VMEM((HEAD_DIM, 2 * T), F32)
    return pl.pallas_call(
        _moba_attn_kernel,
        grid=(bsz, N_PAIRS, nb),
        in_specs=_attn_in_specs(nb) + [
            pl.BlockSpec((1, 1, 1, nb, 2 * T), lambda b, p, i: (b, p, i, 0, 0)),
        ],
        out_specs=pl.BlockSpec((1, T, PAIR), lambda b, p, i: (b, i, p)),
        out_shape=jax.ShapeDtypeStruct((bsz, nb * T, D_MODEL), BF16),
        scratch_shapes=scratch,
        compiler_params=_params(3),
        name="moba_attn",
    )(qt, k, vt, bias, mask)


def _outproj_ln_kernel(o_ref, h_ref, w_ref, g_ref, b_ref, out_ref):
    y = ALPHA * h_ref[...] + jnp.dot(o_ref[...], w_ref[...], preferred_element_type=F32)
    out_ref[...] = _layer_norm(y, g_ref[...], b_ref[...])


def _outproj_ln(o, h, w, g, b):
    m = h.shape[0]
    return pl.pallas_call(
        _outproj_ln_kernel,
        grid=(m // TM,),
        in_specs=[
            pl.BlockSpec((TM, D_MODEL), lambda i: (i, 0)),
            pl.BlockSpec((TM, D_MODEL), lambda i: (i, 0)),
            _weight_spec((D_MODEL, D_MODEL)),
            _weight_spec((1, D_MODEL)),
            _weight_spec((1, D_MODEL)),
        ],
        out_specs=pl.BlockSpec((TM, D_MODEL), lambda i: (i, 0)),
        out_shape=jax.ShapeDtypeStruct((m, D_MODEL), F32),
        compiler_params=_params(1),
        name="outproj_ln",
    )(o, h, w, g, b)


def _mlp_ln_kernel(h_ref, wup_ref, wdn_ref, g_ref, b_ref, out_ref):
    h = h_ref[...]
    hb = h.astype(BF16)
    y = ALPHA * h
    for c in range(D_FF // D_MODEL):
        cols = slice(c * D_MODEL, (c + 1) * D_MODEL)
        u = jnp.dot(hb, wup_ref[:, cols], preferred_element_type=F32)
        u = jnp.square(jnp.maximum(u, 0.0)).astype(BF16)
        y = y + jnp.dot(u, wdn_ref[cols, :], preferred_element_type=F32)
    out_ref[...] = _layer_norm(y, g_ref[...], b_ref[...])


def _mlp_ln(h, wup, wdn, g, b):
    m = h.shape[0]
    return pl.pallas_call(
        _mlp_ln_kernel,
        grid=(m // TM,),
        in_specs=[
            pl.BlockSpec((TM, D_MODEL), lambda i: (i, 0)),
            _weight_spec((D_MODEL, D_FF)),
            _weight_spec((D_FF, D_MODEL)),
            _weight_spec((1, D_MODEL)),
            _weight_spec((1, D_MODEL)),
        ],
        out_specs=pl.BlockSpec((TM, D_MODEL), lambda i: (i, 0)),
        out_shape=jax.ShapeDtypeStruct((m, D_MODEL), F32),
        compiler_params=_params(1),
        name="mlp_ln",
    )(h, wup, wdn, g, b)


def kernel(x, w_in_a, lam_a, subln_a, w_out_a, w_kv_shared, w_q_b, w_out_b, rel_bias,
           ln1_g, ln1_b, ln2_g, ln2_b, w_up, w_down):
    bsz, seq, d = x.shape
    assert d == D_MODEL and seq % TM == 0 and TM % T == 0
    assert w_in_a.shape[0] == 1 and w_q_b.shape[0] == 1 and w_up.shape[0] == DEPTH
    m = bsz * seq
    row = lambda v: v.reshape(1, D_MODEL)

    bias = _bias_tiles(rel_bias)

    w_in = w_in_a[0]
    wqt = w_in[:, 0:D_MODEL].T.astype(BF16)
    wk = w_in[:, D_MODEL:2 * D_MODEL].astype(BF16)
    wvt = w_in[:, 2 * D_MODEL:].T.astype(BF16)
    qt, k, vt = _proj_a(x, wqt, wk, wvt)
    o = _diff_attn(qt, k, vt, bias, lam_a[0], subln_a[0].reshape(PAIR, 1))
    h = _outproj_ln(o.reshape(m, d), x.reshape(m, d), w_out_a[0].astype(BF16),
                    row(ln1_g[0]), row(ln1_b[0]))
    h = _mlp_ln(h, w_up[0].astype(BF16), w_down[0].astype(BF16), row(ln2_g[0]), row(ln2_b[0]))

    h3 = h.reshape(bsz, seq, d)
    k2, vt2, kmean = _proj_kv(h3, w_kv_shared[:, 0:D_MODEL].astype(BF16),
                              w_kv_shared[:, D_MODEL:].T.astype(BF16))
    qt2, mask = _proj_q_gate(h3, w_q_b[0].T.astype(BF16), kmean)
    o = _moba_attn(qt2, k2, vt2, bias, mask)
    h = _outproj_ln(o.reshape(m, d), h, w_out_b[0].astype(BF16), row(ln1_g[1]), row(ln1_b[1]))
    h = _mlp_ln(h, w_up[1].astype(BF16), w_down[1].astype(BF16), row(ln2_g[1]), row(ln2_b[1]))
    return h.reshape(bsz, seq, d)
```

```python
import functools
import math

import jax
import jax.numpy as jnp
from jax import lax
from jax.experimental import pallas as pl
from jax.experimental.pallas import tpu as pltpu

F32 = jnp.float32
BF16 = jnp.bfloat16

D_MODEL = 1024
D_FF = 4 * D_MODEL
DEPTH = 2
HEAD_DIM = 64
PAIR = 2 * HEAD_DIM
N_PAIRS = D_MODEL // PAIR
T = 256
TQ = 2 * T
COLS = 2 * TQ
N_CHUNKS = COLS // T
BF16_ROWS = 16
MOBA_TOPK = 3
REL_BUCKETS = 32
REL_MAX_DIST = 128
LN_EPS = 1e-5
ALPHA = (2 * DEPTH) ** 0.25
LAM_INIT_A = 0.8 - 0.6 * math.exp(-0.3 * 0)
LOG2E = 1.0 / math.log(2.0)
Q_SCALE = HEAD_DIM ** -0.5 * LOG2E
NEG = -1e30
V7X_VMEM_LIMIT = 56 * 1024 * 1024
TM = TQ
DV_A = PAIR
VT_ROWS_A = DV_A + BF16_ROWS
VT_ROWS_B = HEAD_DIM + BF16_ROWS


def _params(n_axes):
    return pltpu.CompilerParams(
        dimension_semantics=("arbitrary",) * n_axes,
        vmem_limit_bytes=V7X_VMEM_LIMIT,
    )


def _nt_dot(a, b):
    return lax.dot_general(a, b, (((1,), (1,)), ((), ())), preferred_element_type=F32)


def _layer_norm(y, g, b):
    mu = jnp.mean(y, axis=-1, keepdims=True)
    yc = y - mu
    var = jnp.mean(yc * yc, axis=-1, keepdims=True)
    return yc * lax.rsqrt(var + LN_EPS) * g + b


def _chunk(c):
    return slice(c * T, (c + 1) * T)


def _bias_tile_kernel(tab_ref, out_ref):
    pr = pl.program_id(0)
    j = lax.broadcasted_iota(jnp.int32, (T, T), 0)
    i = lax.broadcasted_iota(jnp.int32, (T, T), 1)
    max_exact = REL_BUCKETS // 2
    for o in range(2):
        d = i - j + o * T
        n = jnp.maximum(d, 0)
        nf = jnp.maximum(n, max_exact).astype(F32)
        large = max_exact + (jnp.log(nf / max_exact) / math.log(REL_MAX_DIST / max_exact)
                             * (REL_BUCKETS - max_exact)).astype(jnp.int32)
        large = jnp.minimum(large, REL_BUCKETS - 1)
        bucket = jnp.where(n < max_exact, n, large)
        for m in range(2):
            col = 2 * pr + m
            far = tab_ref[REL_BUCKETS - 1, col]
            val = jnp.zeros((T, T), F32)
            for b in range(REL_BUCKETS - 1):
                val = jnp.where(bucket == b, (tab_ref[b, col] - far) * LOG2E, val)
            if o == 0:
                val = jnp.where(d >= 0, val, NEG)
            out_ref[0, o, m] = val


def _bias_tiles(rel_bias):
    return pl.pallas_call(
        _bias_tile_kernel,
        grid=(N_PAIRS,),
        in_specs=[pl.BlockSpec(memory_space=pltpu.SMEM)],
        out_specs=pl.BlockSpec((1, 2, 2, T, T), lambda p: (p, 0, 0, 0, 0)),
        out_shape=jax.ShapeDtypeStruct((N_PAIRS, 2, 2, T, T), F32),
        compiler_params=_params(1),
        name="bias_tiles",
    )(rel_bias)


def _store_q_tile(ref, qt):
    for p in range(N_PAIRS):
        ref[0, p, 0] = qt[p * PAIR:(p + 1) * PAIR].astype(BF16)


def _store_k_blocks(ref, k):
    for t in range(TM // T):
        for p in range(N_PAIRS):
            ref[0, p, t] = k[t * T:(t + 1) * T, p * PAIR:(p + 1) * PAIR].astype(BF16)


def _store_vt_blocks(ref, vt, width):
    groups = PAIR // width
    ones = jnp.ones((BF16_ROWS, T), BF16)
    for t in range(TM // T):
        for p in range(N_PAIRS):
            for g in range(groups):
                r0 = g * (width + BF16_ROWS)
                rows = slice(p * PAIR + g * width, p * PAIR + (g + 1) * width)
                ref[0, p, t, r0:r0 + width] = vt[rows, t * T:(t + 1) * T].astype(BF16)
                ref[0, p, t, r0 + width:r0 + width + BF16_ROWS] = ones


def _proj_a_kernel(x_ref, wqt_ref, wk_ref, wvt_ref, qt_ref, k_ref, vt_ref):
    xb = x_ref[0].astype(BF16)
    _store_q_tile(qt_ref, _nt_dot(wqt_ref[...], xb) * Q_SCALE)
    _store_k_blocks(k_ref, jnp.dot(xb, wk_ref[...], preferred_element_type=F32))
    _store_vt_blocks(vt_ref, _nt_dot(wvt_ref[...], xb), DV_A)


def _q_tile_spec():
    return pl.BlockSpec((1, N_PAIRS, 1, PAIR, TQ), lambda b, i: (b, 0, i, 0, 0))


def _k_blocks_spec():
    return pl.BlockSpec((1, N_PAIRS, TM // T, T, PAIR), lambda b, i: (b, 0, i, 0, 0))


def _vt_blocks_spec(rows):
    return pl.BlockSpec((1, N_PAIRS, TM // T, rows, T), lambda b, i: (b, 0, i, 0, 0))


def _weight_spec(shape):
    return pl.BlockSpec(shape, lambda *_: (0,) * len(shape))


def _proj_a(x, wqt, wk, wvt):
    bsz, seq, _ = x.shape
    nb = seq // T
    return pl.pallas_call(
        _proj_a_kernel,
        grid=(bsz, seq // TM),
        in_specs=[
            pl.BlockSpec((1, TM, D_MODEL), lambda b, i: (b, i, 0)),
            _weight_spec((D_MODEL, D_MODEL)),
            _weight_spec((D_MODEL, D_MODEL)),
            _weight_spec((D_MODEL, D_MODEL)),
        ],
        out_specs=[_q_tile_spec(), _k_blocks_spec(), _vt_blocks_spec(VT_ROWS_A)],
        out_shape=[
            jax.ShapeDtypeStruct((bsz, N_PAIRS, seq // TQ, PAIR, TQ), BF16),
            jax.ShapeDtypeStruct((bsz, N_PAIRS, nb, T, PAIR), BF16),
            jax.ShapeDtypeStruct((bsz, N_PAIRS, nb, VT_ROWS_A, T), BF16),
        ],
        compiler_params=_params(2),
        name="proj_a",
    )(x, wqt, wk, wvt)


def _proj_kv_kernel(h_ref, wk_ref, wvt_ref, k_ref, vt_ref, kmean_ref):
    hb = h_ref[0].astype(BF16)
    k = jnp.dot(hb, wk_ref[...], preferred_element_type=F32)
    _store_k_blocks(k_ref, k)
    for t in range(TM // T):
        blk = pl.program_id(1) * (TM // T) + t
        kmean_ref[0, pl.ds(blk, 1), :] = jnp.mean(k[t * T:(t + 1) * T], axis=0, keepdims=True)
    _store_vt_blocks(vt_ref, _nt_dot(wvt_ref[...], hb), HEAD_DIM)


def _proj_kv(h, wk, wvt):
    bsz, seq, _ = h.shape
    nb = seq // T
    return pl.pallas_call(
        _proj_kv_kernel,
        grid=(bsz, seq // TM),
        in_specs=[
            pl.BlockSpec((1, TM, D_MODEL), lambda b, i: (b, i, 0)),
            _weight_spec((D_MODEL, D_MODEL)),
            _weight_spec((D_MODEL, D_MODEL)),
        ],
        out_specs=[
            _k_blocks_spec(),
            _vt_blocks_spec(2 * VT_ROWS_B),
            pl.BlockSpec((1, nb, D_MODEL), lambda b, i: (b, 0, 0)),
        ],
        out_shape=[
            jax.ShapeDtypeStruct((bsz, N_PAIRS, nb, T, PAIR), BF16),
            jax.ShapeDtypeStruct((bsz, N_PAIRS, nb, 2 * VT_ROWS_B, T), BF16),
            jax.ShapeDtypeStruct((bsz, nb, D_MODEL), F32),
        ],
        compiler_params=_params(2),
        name="proj_kv",
    )(h, wk, wvt)


def _proj_q_gate_kernel(h_ref, wqt_ref, kmean_ref, qt_ref, mask_ref, *, n_blocks):
    hb = h_ref[0].astype(BF16)
    qt = _nt_dot(wqt_ref[...], hb)
    _store_q_tile(qt_ref, qt * Q_SCALE)

    km = kmean_ref[0]
    lane = lax.broadcasted_iota(jnp.int32, (n_blocks, PAIR), 1)
    blk = lax.broadcasted_iota(jnp.int32, (n_blocks, TM), 0)
    col = lax.broadcasted_iota(jnp.int32, (1, TM), 1)
    q_blk = pl.program_id(1) * (TM // T) + lax.shift_right_logical(col, T.bit_length() - 1)
    past = blk < q_blk
    blk_f = blk.astype(F32)
    for p in range(N_PAIRS):
        km_p = km[:, p * PAIR:(p + 1) * PAIR]
        km2 = jnp.concatenate(
            [jnp.where(lane < HEAD_DIM, km_p, 0.0), jnp.where(lane >= HEAD_DIM, km_p, 0.0)],
            axis=0)
        g2 = jnp.dot(km2, qt[p * PAIR:(p + 1) * PAIR], preferred_element_type=F32,
                     precision=lax.Precision.HIGHEST)
        for sub in range(2):
            g = jnp.where(past, g2[sub * n_blocks:(sub + 1) * n_blocks], -jnp.inf)
            keep = jnp.zeros((n_blocks, TM), F32)
            for _ in range(MOBA_TOPK):
                top = jnp.max(g, axis=0, keepdims=True)
                first = jnp.min(jnp.where(g == top, blk_f, float(n_blocks)), axis=0, keepdims=True)
                pick = (blk_f == first) & (top > -jnp.inf)
                keep = jnp.where(pick, 1.0, keep)
                g = jnp.where(pick, -jnp.inf, g)
            mask_ref[0, p, 0, :, sub * TQ:(sub + 1) * TQ] = jnp.where(keep > 0.0, 0.0, NEG)


def _proj_q_gate(h, wqt, kmean):
    bsz, seq, _ = h.shape
    nb = seq // T
    return pl.pallas_call(
        functools.partial(_proj_q_gate_kernel, n_blocks=nb),
        grid=(bsz, seq // TM),
        in_specs=[
            pl.BlockSpec((1, TM, D_MODEL), lambda b, i: (b, i, 0)),
            _weight_spec((D_MODEL, D_MODEL)),
            pl.BlockSpec((1, nb, D_MODEL), lambda b, i: (b, 0, 0)),
        ],
        out_specs=[
            _q_tile_spec(),
            pl.BlockSpec((1, N_PAIRS, 1, nb, COLS), lambda b, i: (b, 0, i, 0, 0)),
        ],
        out_shape=[
            jax.ShapeDtypeStruct((bsz, N_PAIRS, seq // TQ, PAIR, TQ), BF16),
            jax.ShapeDtypeStruct((bsz, N_PAIRS, seq // TQ, nb, COLS), F32),
        ],
        compiler_params=_params(2),
        name="proj_q_gate",
    )(h, wqt, kmean)


class _AttnRefs:
    def __init__(self, k_ref, vt_ref, q2, s, mb, p, al, m, acc):
        self.k, self.vt, self.q2 = k_ref, vt_ref, q2
        self.s, self.mb, self.p, self.al, self.m, self.acc = s, mb, p, al, m, acc


def _fill_q2(q2_ref, qt):
    zeros = jnp.zeros((HEAD_DIM, TQ), BF16)
    q2_ref[0:HEAD_DIM, 0:TQ] = qt[0:HEAD_DIM]
    q2_ref[0:HEAD_DIM, TQ:COLS] = zeros
    q2_ref[HEAD_DIM:PAIR, 0:TQ] = zeros
    q2_ref[HEAD_DIM:PAIR, TQ:COLS] = qt[HEAD_DIM:PAIR]


def _stage_a(r, par, blk, extras):
    kj = r.k[0, 0, blk]
    for c in range(N_CHUNKS):
        if isinstance(extras[c], str):
            s = jnp.full((T, T), NEG, F32)
        else:
            s = jnp.dot(kj, r.q2[:, _chunk(c)], preferred_element_type=F32)
            if extras[c] is not None:
                s = s + extras[c]
        r.s[par][:, _chunk(c)] = s
        r.mb[par][:, _chunk(c)] = jnp.max(s, axis=0, keepdims=True)


def _stage_b(r, par):
    for c in range(N_CHUNKS):
        m_old = r.m[:, _chunk(c)]
        m_new = jnp.maximum(m_old, r.mb[par][:, _chunk(c)])
        r.al[par][:, _chunk(c)] = jnp.exp2(m_old - m_new)
        r.m[:, _chunk(c)] = m_new
        r.p[par][:, _chunk(c)] = jnp.exp2(r.s[par][:, _chunk(c)] - m_new).astype(BF16)


def _stage_c(r, par, blk, vt_rows):
    vt = r.vt[0, 0, blk]
    for c in range(N_CHUNKS):
        pv = jnp.dot(vt[vt_rows[c]], r.p[par][:, _chunk(c)], preferred_element_type=F32)
        r.acc[:, _chunk(c)] = r.al[par][:, _chunk(c)] * r.acc[:, _chunk(c)] + pv


def _attend(r, tile, vt_rows, extras0, extras1, extras2, generic_extras):
    r.m[...] = jnp.full(r.m.shape, NEG, F32)
    r.acc[...] = jnp.zeros(r.acc.shape, F32)
    blk_a, blk_b = 2 * tile, 2 * tile + 1

    _stage_a(r, 0, blk_a, extras0)
    _stage_a(r, 1, blk_b, extras1)
    _stage_b(r, 0)

    @pl.when(tile > 0)
    def _():
        _stage_a(r, 0, blk_a - 1, extras2())
        _stage_b(r, 1)
        _stage_c(r, 0, blk_a, vt_rows)

        _stage_a(r, 1, 0, generic_extras(0))
        _stage_b(r, 0)
        _stage_c(r, 1, blk_b, vt_rows)

        def body(t, carry):
            first = 1 + 2 * t
            prev0 = jnp.where(t == 0, blk_a - 1, first - 2)
            _stage_a(r, 0, first, generic_extras(first))
            _stage_b(r, 1)
            _stage_c(r, 0, prev0, vt_rows)
            _stage_a(r, 1, first + 1, generic_extras(first + 1))
            _stage_b(r, 0)
            _stage_c(r, 1, first - 1, vt_rows)
            return carry

        lax.fori_loop(0, tile - 1, body, 0)

    last0 = jnp.where(tile == 0, blk_a, jnp.where(tile == 1, blk_a - 1, blk_a - 3))
    last1 = jnp.where(tile == 0, blk_b, blk_a - 2)
    _stage_b(r, 1)
    _stage_c(r, 0, last0, vt_rows)
    _stage_c(r, 1, last1, vt_rows)


def _attn_scratch(acc_rows):
    return [
        pltpu.VMEM((PAIR, COLS), BF16),
        pltpu.VMEM((T, COLS), F32),
        pltpu.VMEM((T, COLS), F32),
        pltpu.VMEM((1, COLS), F32),
        pltpu.VMEM((1, COLS), F32),
        pltpu.VMEM((T, COLS), BF16),
        pltpu.VMEM((T, COLS), BF16),
        pltpu.VMEM((1, COLS), F32),
        pltpu.VMEM((1, COLS), F32),
        pltpu.VMEM((1, COLS), F32),
        pltpu.VMEM((acc_rows, COLS), F32),
    ]


def _make_refs(k_ref, vt_ref, scratch):
    q2, s0, s1, mb0, mb1, p0, p1, al0, al1, m, acc = scratch
    return _AttnRefs(k_ref, vt_ref, q2, (s0, s1), (mb0, mb1), (p0, p1), (al0, al1), m, acc)


def _attn_in_specs(nb, vt_rows):
    return [
        pl.BlockSpec((1, 1, 1, PAIR, TQ), lambda b, p, i: (b, p, i, 0, 0)),
        pl.BlockSpec((1, 1, nb, T, PAIR), lambda b, p, i: (b, p, 0, 0, 0)),
        pl.BlockSpec((1, 1, nb, vt_rows, T), lambda b, p, i: (b, p, 0, 0, 0)),
        pl.BlockSpec((1, 2, 2, T, T), lambda b, p, i: (p, 0, 0, 0, 0)),
    ]


def _diff_attn_kernel(qt_ref, k_ref, vt_ref, bias_ref, lam_ref, subln_ref, o_ref, *scratch):
    r = _make_refs(k_ref, vt_ref, scratch)
    _fill_q2(r.q2, qt_ref[0, 0, 0])
    rows = [slice(0, VT_ROWS_A)] * N_CHUNKS
    extras0 = [bias_ref[0, 0, 0], bias_ref[0, 1, 0], bias_ref[0, 0, 1], bias_ref[0, 1, 1]]
    extras1 = ["future", bias_ref[0, 0, 0], "future", bias_ref[0, 0, 1]]
    extras2 = lambda: [bias_ref[0, 1, 0], None, bias_ref[0, 1, 1], None]
    _attend(r, pl.program_id(2), rows, extras0, extras1, extras2, lambda blk: [None] * N_CHUNKS)

    lp = lam_ref[...]
    lam = (jnp.exp(jnp.sum(lp[0:1] * lp[1:2], axis=1, keepdims=True))
           - jnp.exp(jnp.sum(lp[2:3] * lp[3:4], axis=1, keepdims=True)) + LAM_INIT_A)
    o_all = r.acc[0:DV_A] / r.acc[DV_A:DV_A + 1]
    o = o_all[:, 0:TQ] - lam * o_all[:, TQ:COLS]
    o = o * lax.rsqrt(jnp.mean(o * o, axis=0, keepdims=True) + LN_EPS)
    o = o * subln_ref[...] * (1.0 - LAM_INIT_A)
    o_ref[0] = o.T.astype(BF16)


def _diff_attn(qt, k, vt, bias, lam_p, subln_col):
    bsz, _, nq, _, _ = qt.shape
    nb = k.shape[2]
    return pl.pallas_call(
        _diff_attn_kernel,
        grid=(bsz, N_PAIRS, nq),
        in_specs=_attn_in_specs(nb, VT_ROWS_A) + [
            pl.BlockSpec((4, HEAD_DIM), lambda b, p, i: (0, 0)),
            pl.BlockSpec((PAIR, 1), lambda b, p, i: (0, 0)),
        ],
        out_specs=pl.BlockSpec((1, TQ, PAIR), lambda b, p, i: (b, i, p)),
        out_shape=jax.ShapeDtypeStruct((bsz, nq * TQ, D_MODEL), BF16),
        scratch_shapes=_attn_scratch(VT_ROWS_A),
        compiler_params=_params(3),
        name="diff_attn",
    )(qt, k, vt, bias, lam_p, subln_col)


def _moba_attn_kernel(qt_ref, k_ref, vt_ref, bias_ref, mask_ref, o_ref, *scratch):
    r = _make_refs(k_ref, vt_ref, scratch)
    tile = pl.program_id(2)
    _fill_q2(r.q2, qt_ref[0, 0, 0])
    rows = [slice(0, VT_ROWS_B)] * 2 + [slice(VT_ROWS_B, 2 * VT_ROWS_B)] * 2

    def mask_row(blk, c):
        return mask_ref[0, 0, 0, pl.ds(blk, 1), _chunk(c)]

    extras0 = [bias_ref[0, 0, 0], bias_ref[0, 1, 0] + mask_row(2 * tile, 1),
               bias_ref[0, 0, 1], bias_ref[0, 1, 1] + mask_row(2 * tile, 3)]
    extras1 = ["future", bias_ref[0, 0, 0], "future", bias_ref[0, 0, 1]]
    extras2 = lambda: [bias_ref[0, 1, 0] + mask_row(2 * tile - 1, 0), mask_row(2 * tile - 1, 1),
                       bias_ref[0, 1, 1] + mask_row(2 * tile - 1, 2), mask_row(2 * tile - 1, 3)]
    _attend(r, tile, rows, extras0, extras1, extras2,
            lambda blk: [mask_row(blk, c) for c in range(N_CHUNKS)])

    o_all = r.acc[0:HEAD_DIM] / r.acc[HEAD_DIM:HEAD_DIM + 1]
    o = jnp.concatenate([o_all[:, 0:TQ], o_all[:, TQ:COLS]], axis=0)
    o_ref[0] = o.T.astype(BF16)


def _moba_attn(qt, k, vt, bias, mask):
    bsz, _, nq, _, _ = qt.shape
    nb = k.shape[2]
    return pl.pallas_call(
        _moba_attn_kernel,
        grid=(bsz, N_PAIRS, nq),
        in_specs=_attn_in_specs(nb, 2 * VT_ROWS_B) + [
            pl.BlockSpec((1, 1, 1, nb, COLS), lambda b, p, i: (b, p, i, 0, 0)),
        ],
        out_specs=pl.BlockSpec((1, TQ, PAIR), lambda b, p, i: (b, i, p)),
        out_shape=jax.ShapeDtypeStruct((bsz, nq * TQ, D_MODEL), BF16),
        scratch_shapes=_attn_scratch(VT_ROWS_B),
        compiler_params=_params(3),
        name="moba_attn",
    )(qt, k, vt, bias, mask)


def _outproj_ln_kernel(o_ref, h_ref, w_ref, g_ref, b_ref, out_ref):
    y = ALPHA * h_ref[...] + jnp.dot(o_ref[...], w_ref[...], preferred_element_type=F32)
    out_ref[...] = _layer_norm(y, g_ref[...], b_ref[...])


def _outproj_ln(o, h, w, g, b):
    m = h.shape[0]
    return pl.pallas_call(
        _outproj_ln_kernel,
        grid=(m // TM,),
        in_specs=[
            pl.BlockSpec((TM, D_MODEL), lambda i: (i, 0)),
            pl.BlockSpec((TM, D_MODEL), lambda i: (i, 0)),
            _weight_spec((D_MODEL, D_MODEL)),
            _weight_spec((1, D_MODEL)),
            _weight_spec((1, D_MODEL)),
        ],
        out_specs=pl.BlockSpec((TM, D_MODEL), lambda i: (i, 0)),
        out_shape=jax.ShapeDtypeStruct((m, D_MODEL), F32),
        compiler_params=_params(1),
        name="outproj_ln",
    )(o, h, w, g, b)


def _mlp_ln_kernel(h_ref, wup_ref, wdn_ref, g_ref, b_ref, out_ref):
    h = h_ref[...]
    hb = h.astype(BF16)
    y = ALPHA * h
    for c in range(D_FF // D_MODEL):
        cols = slice(c * D_MODEL, (c + 1) * D_MODEL)
        u = jnp.dot(hb, wup_ref[:, cols], preferred_element_type=F32)
        u = jnp.square(jnp.maximum(u, 0.0)).astype(BF16)
        y = y + jnp.dot(u, wdn_ref[cols, :], preferred_element_type=F32)
    out_ref[...] = _layer_norm(y, g_ref[...], b_ref[...])


def _mlp_ln(h, wup, wdn, g, b):
    m = h.shape[0]
    return pl.pallas_call(
        _mlp_ln_kernel,
        grid=(m // TM,),
        in_specs=[
            pl.BlockSpec((TM, D_MODEL), lambda i: (i, 0)),
            _weight_spec((D_MODEL, D_FF)),
            _weight_spec((D_FF, D_MODEL)),
            _weight_spec((1, D_MODEL)),
            _weight_spec((1, D_MODEL)),
        ],
        out_specs=pl.BlockSpec((TM, D_MODEL), lambda i: (i, 0)),
        out_shape=jax.ShapeDtypeStruct((m, D_MODEL), F32),
        compiler_params=_params(1),
        name="mlp_ln",
    )(h, wup, wdn, g, b)


def kernel(x, w_in_a, lam_a, subln_a, w_out_a, w_kv_shared, w_q_b, w_out_b, rel_bias,
           ln1_g, ln1_b, ln2_g, ln2_b, w_up, w_down):
    bsz, seq, d = x.shape
    assert d == D_MODEL and seq % TM == 0 and TM % T == 0
    assert w_in_a.shape[0] == 1 and w_q_b.shape[0] == 1 and w_up.shape[0] == DEPTH
    m = bsz * seq
    row = lambda v: v.reshape(1, D_MODEL)

    bias = _bias_tiles(rel_bias)

    w_in = w_in_a[0]
    wqt = w_in[:, 0:D_MODEL].T.astype(BF16)
    wk = w_in[:, D_MODEL:2 * D_MODEL].astype(BF16)
    wvt = w_in[:, 2 * D_MODEL:].T.astype(BF16)
    qt, k, vt = _proj_a(x, wqt, wk, wvt)
    o = _diff_attn(qt, k, vt, bias, lam_a[0], subln_a[0].reshape(PAIR, 1))
    h = _outproj_ln(o.reshape(m, d), x.reshape(m, d), w_out_a[0].astype(BF16),
                    row(ln1_g[0]), row(ln1_b[0]))
    h = _mlp_ln(h, w_up[0].astype(BF16), w_down[0].astype(BF16), row(ln2_g[0]), row(ln2_b[0]))

    h3 = h.reshape(bsz, seq, d)
    k2, vt2, kmean = _proj_kv(h3, w_kv_shared[:, 0:D_MODEL].astype(BF16),
                              w_kv_shared[:, D_MODEL:].T.astype(BF16))
    qt2, mask = _proj_q_gate(h3, w_q_b[0].T.astype(BF16), kmean)
    o = _moba_attn(qt2, k2, vt2, bias, mask)
    h = _outproj_ln(o.reshape(m, d), h, w_out_b[0].astype(BF16), row(ln1_g[1]), row(ln1_b[1]))
    h = _mlp_ln(h, w_up[1].astype(BF16), w_down[1].astype(BF16), row(ln2_g[1]), row(ln2_b[1]))
    return h.reshape(bsz, seq, d)
```

```python
import functools
import math

import jax
import jax.numpy as jnp
from jax import lax
from jax.experimental import pallas as pl
from jax.experimental.pallas import tpu as pltpu

F32 = jnp.float32
BF16 = jnp.bfloat16

D_MODEL = 1024
D_FF = 4 * D_MODEL
DEPTH = 2
HEAD_DIM = 64
PAIR = 2 * HEAD_DIM
N_PAIRS = D_MODEL // PAIR
T = 256
TQS = 2 * T
N_SUB = 4
TQ = N_SUB * T
N_QT = TQ // TQS
COLS = 2 * TQ
N_CHUNKS = 2 * N_SUB
LANES = 128
COLS_PADDED = COLS + LANES
BF16_ROWS = 16
MOBA_TOPK = 3
REL_BUCKETS = 32
REL_MAX_DIST = 128
LN_EPS = 1e-5
ALPHA = (2 * DEPTH) ** 0.25
LAM_INIT_A = 0.8 - 0.6 * math.exp(-0.3 * 0)
LOG2E = 1.0 / math.log(2.0)
Q_SCALE = HEAD_DIM ** -0.5 * LOG2E
NEG = -1e30
V7X_VMEM_LIMIT = 56 * 1024 * 1024
TM = TQS
DV_A = PAIR
VT_ROWS_A = DV_A + BF16_ROWS
VT_ROWS_B = HEAD_DIM + BF16_ROWS


def _params(n_axes, flags=None):
    return pltpu.CompilerParams(
        dimension_semantics=("arbitrary",) * n_axes,
        vmem_limit_bytes=V7X_VMEM_LIMIT,
        flags=flags,
    )


def _nt_dot(a, b):
    return lax.dot_general(a, b, (((1,), (1,)), ((), ())), preferred_element_type=F32)


def _layer_norm(y, g, b):
    mu = jnp.mean(y, axis=-1, keepdims=True)
    yc = y - mu
    var = jnp.mean(yc * yc, axis=-1, keepdims=True)
    return yc * lax.rsqrt(var + LN_EPS) * g + b


def _chunk(c):
    return slice(c * T, (c + 1) * T)


def _bias_tile_kernel(tab_ref, out_ref):
    pr = pl.program_id(0)
    j = lax.broadcasted_iota(jnp.int32, (T, T), 0)
    i = lax.broadcasted_iota(jnp.int32, (T, T), 1)
    max_exact = REL_BUCKETS // 2
    for o in range(2):
        d = i - j + o * T
        n = jnp.maximum(d, 0)
        nf = jnp.maximum(n, max_exact).astype(F32)
        large = max_exact + (jnp.log(nf / max_exact) / math.log(REL_MAX_DIST / max_exact)
                             * (REL_BUCKETS - max_exact)).astype(jnp.int32)
        large = jnp.minimum(large, REL_BUCKETS - 1)
        bucket = jnp.where(n < max_exact, n, large)
        for m in range(2):
            col = 2 * pr + m
            far = tab_ref[REL_BUCKETS - 1, col]
            val = jnp.zeros((T, T), F32)
            for b in range(REL_BUCKETS - 1):
                val = jnp.where(bucket == b, (tab_ref[b, col] - far) * LOG2E, val)
            if o == 0:
                val = jnp.where(d >= 0, val, NEG)
            out_ref[0, o, m] = val


def _bias_tiles(rel_bias):
    return pl.pallas_call(
        _bias_tile_kernel,
        grid=(N_PAIRS,),
        in_specs=[pl.BlockSpec(memory_space=pltpu.SMEM)],
        out_specs=pl.BlockSpec((1, 2, 2, T, T), lambda p: (p, 0, 0, 0, 0)),
        out_shape=jax.ShapeDtypeStruct((N_PAIRS, 2, 2, T, T), F32),
        compiler_params=_params(1),
        name="bias_tiles",
    )(rel_bias)


def _store_q_tile(ref, qt):
    for p in range(N_PAIRS):
        ref[0, p, 0] = qt[p * PAIR:(p + 1) * PAIR].astype(BF16)


def _store_k_blocks(ref, k):
    for t in range(TM // T):
        for p in range(N_PAIRS):
            ref[0, p, t] = k[t * T:(t + 1) * T, p * PAIR:(p + 1) * PAIR].astype(BF16)


def _store_vt_blocks(ref, vt, width):
    groups = PAIR // width
    ones = jnp.ones((BF16_ROWS, T), BF16)
    for t in range(TM // T):
        for p in range(N_PAIRS):
            for g in range(groups):
                r0 = g * (width + BF16_ROWS)
                rows = slice(p * PAIR + g * width, p * PAIR + (g + 1) * width)
                ref[0, p, t, r0:r0 + width] = vt[rows, t * T:(t + 1) * T].astype(BF16)
                ref[0, p, t, r0 + width:r0 + width + BF16_ROWS] = ones


def _proj_a_kernel(x_ref, wqt_ref, wk_ref, wvt_ref, qt_ref, k_ref, vt_ref):
    xb = x_ref[0].astype(BF16)
    _store_q_tile(qt_ref, _nt_dot(wqt_ref[...], xb) * Q_SCALE)
    _store_k_blocks(k_ref, jnp.dot(xb, wk_ref[...], preferred_element_type=F32))
    _store_vt_blocks(vt_ref, _nt_dot(wvt_ref[...], xb), DV_A)


def _q_tile_spec():
    return pl.BlockSpec((1, N_PAIRS, 1, PAIR, TQS), lambda b, i: (b, 0, i, 0, 0))


def _k_blocks_spec():
    return pl.BlockSpec((1, N_PAIRS, TM // T, T, PAIR), lambda b, i: (b, 0, i, 0, 0))


def _vt_blocks_spec(rows):
    return pl.BlockSpec((1, N_PAIRS, TM // T, rows, T), lambda b, i: (b, 0, i, 0, 0))


def _weight_spec(shape):
    return pl.BlockSpec(shape, lambda *_: (0,) * len(shape))


def _proj_a(x, wqt, wk, wvt):
    bsz, seq, _ = x.shape
    nb = seq // T
    return pl.pallas_call(
        _proj_a_kernel,
        grid=(bsz, seq // TM),
        in_specs=[
            pl.BlockSpec((1, TM, D_MODEL), lambda b, i: (b, i, 0)),
            _weight_spec((D_MODEL, D_MODEL)),
            _weight_spec((D_MODEL, D_MODEL)),
            _weight_spec((D_MODEL, D_MODEL)),
        ],
        out_specs=[_q_tile_spec(), _k_blocks_spec(), _vt_blocks_spec(VT_ROWS_A)],
        out_shape=[
            jax.ShapeDtypeStruct((bsz, N_PAIRS, seq // TQS, PAIR, TQS), BF16),
            jax.ShapeDtypeStruct((bsz, N_PAIRS, nb, T, PAIR), BF16),
            jax.ShapeDtypeStruct((bsz, N_PAIRS, nb, VT_ROWS_A, T), BF16),
        ],
        compiler_params=_params(2),
        name="proj_a",
    )(x, wqt, wk, wvt)


def _proj_kv_kernel(h_ref, wk_ref, wvt_ref, k_ref, vt_ref, kmean_ref):
    hb = h_ref[0].astype(BF16)
    k = jnp.dot(hb, wk_ref[...], preferred_element_type=F32)
    _store_k_blocks(k_ref, k)
    for t in range(TM // T):
        blk = pl.program_id(1) * (TM // T) + t
        kmean_ref[0, pl.ds(blk, 1), :] = jnp.mean(k[t * T:(t + 1) * T], axis=0, keepdims=True)
    _store_vt_blocks(vt_ref, _nt_dot(wvt_ref[...], hb), HEAD_DIM)


def _proj_kv(h, wk, wvt):
    bsz, seq, _ = h.shape
    nb = seq // T
    return pl.pallas_call(
        _proj_kv_kernel,
        grid=(bsz, seq // TM),
        in_specs=[
            pl.BlockSpec((1, TM, D_MODEL), lambda b, i: (b, i, 0)),
            _weight_spec((D_MODEL, D_MODEL)),
            _weight_spec((D_MODEL, D_MODEL)),
        ],
        out_specs=[
            _k_blocks_spec(),
            _vt_blocks_spec(2 * VT_ROWS_B),
            pl.BlockSpec((1, nb, D_MODEL), lambda b, i: (b, 0, 0)),
        ],
        out_shape=[
            jax.ShapeDtypeStruct((bsz, N_PAIRS, nb, T, PAIR), BF16),
            jax.ShapeDtypeStruct((bsz, N_PAIRS, nb, 2 * VT_ROWS_B, T), BF16),
            jax.ShapeDtypeStruct((bsz, nb, D_MODEL), F32),
        ],
        compiler_params=_params(2),
        name="proj_kv",
    )(h, wk, wvt)


def _proj_q_gate_kernel(h_ref, wqt_ref, kmean_ref, qt_ref, mask_ref, *, n_blocks):
    hb = h_ref[0].astype(BF16)
    qt = _nt_dot(wqt_ref[...], hb)
    _store_q_tile(qt_ref, qt * Q_SCALE)

    km = kmean_ref[0]
    lane = lax.broadcasted_iota(jnp.int32, (n_blocks, PAIR), 1)
    blk = lax.broadcasted_iota(jnp.int32, (n_blocks, TM), 0)
    col = lax.broadcasted_iota(jnp.int32, (1, TM), 1)
    q_blk = pl.program_id(1) * (TM // T) + lax.shift_right_logical(col, T.bit_length() - 1)
    past = blk < q_blk
    blk_f = blk.astype(F32)
    for p in range(N_PAIRS):
        km_p = km[:, p * PAIR:(p + 1) * PAIR]
        km2 = jnp.concatenate(
            [jnp.where(lane < HEAD_DIM, km_p, 0.0), jnp.where(lane >= HEAD_DIM, km_p, 0.0)],
            axis=0)
        g2 = jnp.dot(km2, qt[p * PAIR:(p + 1) * PAIR], preferred_element_type=F32,
                     precision=lax.Precision.HIGHEST)
        for sub in range(2):
            g = jnp.where(past, g2[sub * n_blocks:(sub + 1) * n_blocks], -jnp.inf)
            keep = jnp.zeros((n_blocks, TM), F32)
            for _ in range(MOBA_TOPK):
                top = jnp.max(g, axis=0, keepdims=True)
                first = jnp.min(jnp.where(g == top, blk_f, float(n_blocks)), axis=0, keepdims=True)
                pick = (blk_f == first) & (top > -jnp.inf)
                keep = jnp.where(pick, 1.0, keep)
                g = jnp.where(pick, -jnp.inf, g)
            mask_ref[0, p, 0, :, sub * TQS:(sub + 1) * TQS] = jnp.where(keep > 0.0, 0.0, NEG)


def _proj_q_gate(h, wqt, kmean):
    bsz, seq, _ = h.shape
    nb = seq // T
    return pl.pallas_call(
        functools.partial(_proj_q_gate_kernel, n_blocks=nb),
        grid=(bsz, seq // TM),
        in_specs=[
            pl.BlockSpec((1, TM, D_MODEL), lambda b, i: (b, i, 0)),
            _weight_spec((D_MODEL, D_MODEL)),
            pl.BlockSpec((1, nb, D_MODEL), lambda b, i: (b, 0, 0)),
        ],
        out_specs=[
            _q_tile_spec(),
            pl.BlockSpec((1, N_PAIRS, 1, nb, 2 * TQS), lambda b, i: (b, 0, i, 0, 0)),
        ],
        out_shape=[
            jax.ShapeDtypeStruct((bsz, N_PAIRS, seq // TQS, PAIR, TQS), BF16),
            jax.ShapeDtypeStruct((bsz, N_PAIRS, seq // TQS, nb, 2 * TQS), F32),
        ],
        compiler_params=_params(2),
        name="proj_q_gate",
    )(h, wqt, kmean)


class _AttnRefs:
    def __init__(self, k_ref, vt_ref, q2, s, mb, p, al, m, acc):
        self.k, self.vt, self.q2 = k_ref, vt_ref, q2
        self.s, self.mb, self.p, self.al, self.m, self.acc = s, mb, p, al, m, acc


def _fill_q2(q2_ref, qt_ref):
    zeros = jnp.zeros((HEAD_DIM, TQ), BF16)
    q2_ref[0:HEAD_DIM, TQ:COLS] = zeros
    q2_ref[HEAD_DIM:PAIR, 0:TQ] = zeros
    for t in range(N_QT):
        qt = qt_ref[0, 0, t]
        q2_ref[0:HEAD_DIM, t * TQS:(t + 1) * TQS] = qt[0:HEAD_DIM]
        q2_ref[HEAD_DIM:PAIR, TQ + t * TQS:TQ + (t + 1) * TQS] = qt[HEAD_DIM:PAIR]


def _stage_a(r, par, blk, extras):
    kj = r.k[0, 0, blk]
    for c in range(N_CHUNKS):
        if isinstance(extras[c], str):
            s = jnp.full((T, T), NEG, F32)
        else:
            s = jnp.dot(kj, r.q2[:, _chunk(c)], preferred_element_type=F32)
            if extras[c] is not None:
                s = s + extras[c]
        r.s[par][:, _chunk(c)] = s
        r.mb[par][:, _chunk(c)] = jnp.max(s, axis=0, keepdims=True)


def _stage_b(r, par):
    for c in range(N_CHUNKS):
        m_old = r.m[:, _chunk(c)]
        m_new = jnp.maximum(m_old, r.mb[par][:, _chunk(c)])
        r.al[par][:, _chunk(c)] = jnp.exp2(m_old - m_new)
        r.m[:, _chunk(c)] = m_new
        r.p[par][:, _chunk(c)] = jnp.exp2(r.s[par][:, _chunk(c)] - m_new).astype(BF16)


def _stage_c(r, par, blk, vt_rows):
    vt = r.vt[0, 0, blk]
    for c in range(N_CHUNKS):
        pv = jnp.dot(vt[vt_rows[c // N_SUB]], r.p[par][:, _chunk(c)], preferred_element_type=F32)
        r.acc[:, _chunk(c)] = r.al[par][:, _chunk(c)] * r.acc[:, _chunk(c)] + pv


def _visit_extras(bias_ref, offset_of, mask_of):
    extras = []
    for c in range(N_CHUNKS):
        g, s = divmod(c, N_SUB)
        off = offset_of(s)
        if off < 0:
            extras.append("future")
            continue
        term = bias_ref[0, off, g] if off <= 1 else None
        mask = mask_of(c) if off >= 1 else None
        if term is None:
            extras.append(mask)
        else:
            extras.append(term if mask is None else term + mask)
    return extras


def _attend(r, tile, vt_rows, bias_ref, mask_row):
    r.m[...] = jnp.full(r.m.shape, NEG, F32)
    r.acc[...] = jnp.zeros(r.acc.shape, F32)
    base = N_SUB * tile
    far = 2

    def mask_of(blk):
        return (lambda c: None) if mask_row is None else (lambda c: mask_row(blk, c))

    def block_of(n):
        return jnp.where(n < N_SUB, base + n, jnp.where(n == N_SUB, base - 1, n - N_SUB - 1))

    def stage(n_static, blk, extras):
        _stage_a(r, n_static % 2, blk, extras)
        if n_static >= 1:
            _stage_b(r, (n_static - 1) % 2)
        if n_static >= 2:
            _stage_c(r, n_static % 2, base + n_static - 2, vt_rows)

    for d in range(N_SUB):
        stage(d, base + d, _visit_extras(bias_ref, lambda s: s - d, mask_of(base + d)))

    @pl.when(tile > 0)
    def _():
        prev = _visit_extras(bias_ref, lambda s: 1 if s == 0 else far, mask_of(base - 1))
        stage(N_SUB, base - 1, prev)
        stage(N_SUB + 1, 0, _visit_extras(bias_ref, lambda s: far, mask_of(0)))

        def body(t, carry):
            n = N_SUB + 2 + 2 * t
            first = 1 + 2 * t
            _stage_a(r, 0, first, _visit_extras(bias_ref, lambda s: far, mask_of(first)))
            _stage_b(r, 1)
            _stage_c(r, 0, block_of(n - 2), vt_rows)
            _stage_a(r, 1, first + 1, _visit_extras(bias_ref, lambda s: far, mask_of(first + 1)))
            _stage_b(r, 0)
            _stage_c(r, 1, first - 1, vt_rows)
            return carry

        lax.fori_loop(0, base // 2 - 1, body, 0)

    n_visits = base + N_SUB
    _stage_b(r, 1)
    _stage_c(r, 0, block_of(n_visits - 2), vt_rows)
    _stage_c(r, 1, block_of(n_visits - 1), vt_rows)


def _attn_scratch(acc_rows):
    return [
        pltpu.VMEM((PAIR, COLS_PADDED), BF16),
        pltpu.VMEM((T, COLS_PADDED), F32),
        pltpu.VMEM((T, COLS_PADDED), F32),
        pltpu.VMEM((1, COLS), F32),
        pltpu.VMEM((1, COLS), F32),
        pltpu.VMEM((T, COLS_PADDED), BF16),
        pltpu.VMEM((T, COLS_PADDED), BF16),
        pltpu.VMEM((1, COLS), F32),
        pltpu.VMEM((1, COLS), F32),
        pltpu.VMEM((1, COLS), F32),
        pltpu.VMEM((acc_rows, COLS_PADDED), F32),
    ]


def _make_refs(k_ref, vt_ref, scratch):
    q2, s0, s1, mb0, mb1, p0, p1, al0, al1, m, acc = scratch
    return _AttnRefs(k_ref, vt_ref, q2, (s0, s1), (mb0, mb1), (p0, p1), (al0, al1), m, acc)


def _attn_in_specs(nb, vt_rows):
    return [
        pl.BlockSpec((1, 1, N_QT, PAIR, TQS), lambda b, p, i: (b, p, i, 0, 0)),
        pl.BlockSpec((1, 1, nb, T, PAIR), lambda b, p, i: (b, p, 0, 0, 0)),
        pl.BlockSpec((1, 1, nb, vt_rows, T), lambda b, p, i: (b, p, 0, 0, 0)),
        pl.BlockSpec((1, 2, 2, T, T), lambda b, p, i: (p, 0, 0, 0, 0)),
    ]


def _diff_attn_kernel(qt_ref, k_ref, vt_ref, bias_ref, lam_ref, subln_ref, o_ref, *scratch):
    r = _make_refs(k_ref, vt_ref, scratch)
    _fill_q2(r.q2, qt_ref)
    _attend(r, pl.program_id(2), [slice(0, VT_ROWS_A)] * 2, bias_ref, None)

    lp = lam_ref[...]
    lam = (jnp.exp(jnp.sum(lp[0:1] * lp[1:2], axis=1, keepdims=True))
           - jnp.exp(jnp.sum(lp[2:3] * lp[3:4], axis=1, keepdims=True)) + LAM_INIT_A)
    o_all = r.acc[0:DV_A, 0:COLS] / r.acc[DV_A:DV_A + 1, 0:COLS]
    o = o_all[:, 0:TQ] - lam * o_all[:, TQ:COLS]
    o = o * lax.rsqrt(jnp.mean(o * o, axis=0, keepdims=True) + LN_EPS)
    o = o * subln_ref[...] * (1.0 - LAM_INIT_A)
    o_ref[0] = o.T.astype(BF16)


def _diff_attn(qt, k, vt, bias, lam_p, subln_col):
    bsz, _, nqs, _, _ = qt.shape
    nb = k.shape[2]
    return pl.pallas_call(
        _diff_attn_kernel,
        grid=(bsz, N_PAIRS, nqs // N_QT),
        in_specs=_attn_in_specs(nb, VT_ROWS_A) + [
            pl.BlockSpec((4, HEAD_DIM), lambda b, p, i: (0, 0)),
            pl.BlockSpec((PAIR, 1), lambda b, p, i: (0, 0)),
        ],
        out_specs=pl.BlockSpec((1, TQ, PAIR), lambda b, p, i: (b, i, p)),
        out_shape=jax.ShapeDtypeStruct((bsz, nqs * TQS, D_MODEL), BF16),
        scratch_shapes=_attn_scratch(VT_ROWS_A),
        compiler_params=_params(3),
        name="diff_attn",
    )(qt, k, vt, bias, lam_p, subln_col)


def _moba_attn_kernel(qt_ref, k_ref, vt_ref, bias_ref, mask_ref, o_ref, *scratch):
    r = _make_refs(k_ref, vt_ref, scratch)
    _fill_q2(r.q2, qt_ref)

    def mask_row(blk, c):
        g, s = divmod(c, N_SUB)
        t, s_in = divmod(s, TQS // T)
        col0 = g * TQS + s_in * T
        return mask_ref[0, 0, t, pl.ds(blk, 1), col0:col0 + T]

    rows = [slice(0, VT_ROWS_B), slice(VT_ROWS_B, 2 * VT_ROWS_B)]
    _attend(r, pl.program_id(2), rows, bias_ref, mask_row)

    o_all = r.acc[0:HEAD_DIM, 0:COLS] / r.acc[HEAD_DIM:HEAD_DIM + 1, 0:COLS]
    o = jnp.concatenate([o_all[:, 0:TQ], o_all[:, TQ:COLS]], axis=0)
    o_ref[0] = o.T.astype(BF16)


def _moba_attn(qt, k, vt, bias, mask):
    bsz, _, nqs, _, _ = qt.shape
    nb = k.shape[2]
    return pl.pallas_call(
        _moba_attn_kernel,
        grid=(bsz, N_PAIRS, nqs // N_QT),
        in_specs=_attn_in_specs(nb, 2 * VT_ROWS_B) + [
            pl.BlockSpec((1, 1, N_QT, nb, 2 * TQS), lambda b, p, i: (b, p, i, 0, 0)),
        ],
        out_specs=pl.BlockSpec((1, TQ, PAIR), lambda b, p, i: (b, i, p)),
        out_shape=jax.ShapeDtypeStruct((bsz, nqs * TQS, D_MODEL), BF16),
        scratch_shapes=_attn_scratch(VT_ROWS_B),
        compiler_params=_params(3),
        name="moba_attn",
    )(qt, k, vt, bias, mask)


def _outproj_ln_kernel(o_ref, h_ref, w_ref, g_ref, b_ref, out_ref):
    y = ALPHA * h_ref[...] + jnp.dot(o_ref[...], w_ref[...], preferred_element_type=F32)
    out_ref[...] = _layer_norm(y, g_ref[...], b_ref[...])


def _outproj_ln(o, h, w, g, b):
    m = h.shape[0]
    return pl.pallas_call(
        _outproj_ln_kernel,
        grid=(m // TM,),
        in_specs=[
            pl.BlockSpec((TM, D_MODEL), lambda i: (i, 0)),
            pl.BlockSpec((TM, D_MODEL), lambda i: (i, 0)),
            _weight_spec((D_MODEL, D_MODEL)),
            _weight_spec((1, D_MODEL)),
            _weight_spec((1, D_MODEL)),
        ],
        out_specs=pl.BlockSpec((TM, D_MODEL), lambda i: (i, 0)),
        out_shape=jax.ShapeDtypeStruct((m, D_MODEL), F32),
        compiler_params=_params(1),
        name="outproj_ln",
    )(o, h, w, g, b)


def _mlp_ln_kernel(h_ref, wup_ref, wdn_ref, g_ref, b_ref, out_ref):
    h = h_ref[...]
    hb = h.astype(BF16)
    y = ALPHA * h
    for c in range(D_FF // D_MODEL):
        cols = slice(c * D_MODEL, (c + 1) * D_MODEL)
        u = jnp.dot(hb, wup_ref[:, cols], preferred_element_type=F32)
        u = jnp.square(jnp.maximum(u, 0.0)).astype(BF16)
        y = y + jnp.dot(u, wdn_ref[cols, :], preferred_element_type=F32)
    out_ref[...] = _layer_norm(y, g_ref[...], b_ref[...])


def _mlp_ln(h, wup, wdn, g, b):
    m = h.shape[0]
    return pl.pallas_call(
        _mlp_ln_kernel,
        grid=(m // TM,),
        in_specs=[
            pl.BlockSpec((TM, D_MODEL), lambda i: (i, 0)),
            _weight_spec((D_MODEL, D_FF)),
            _weight_spec((D_FF, D_MODEL)),
            _weight_spec((1, D_MODEL)),
            _weight_spec((1, D_MODEL)),
        ],
        out_specs=pl.BlockSpec((TM, D_MODEL), lambda i: (i, 0)),
        out_shape=jax.ShapeDtypeStruct((m, D_MODEL), F32),
        compiler_params=_params(1),
        name="mlp_ln",
    )(h, wup, wdn, g, b)


def kernel(x, w_in_a, lam_a, subln_a, w_out_a, w_kv_shared, w_q_b, w_out_b, rel_bias,
           ln1_g, ln1_b, ln2_g, ln2_b, w_up, w_down):
    bsz, seq, d = x.shape
    assert d == D_MODEL and seq % TM == 0 and TM % T == 0
    assert w_in_a.shape[0] == 1 and w_q_b.shape[0] == 1 and w_up.shape[0] == DEPTH
    m = bsz * seq
    row = lambda v: v.reshape(1, D_MODEL)

    bias = _bias_tiles(rel_bias)

    w_in = w_in_a[0]
    wqt = w_in[:, 0:D_MODEL].T.astype(BF16)
    wk = w_in[:, D_MODEL:2 * D_MODEL].astype(BF16)
    wvt = w_in[:, 2 * D_MODEL:].T.astype(BF16)
    qt, k, vt = _proj_a(x, wqt, wk, wvt)
    o = _diff_attn(qt, k, vt, bias, lam_a[0], subln_a[0].reshape(PAIR, 1))
    h = _outproj_ln(o.reshape(m, d), x.reshape(m, d), w_out_a[0].astype(BF16),
                    row(ln1_g[0]), row(ln1_b[0]))
    h = _mlp_ln(h, w_up[0].astype(BF16), w_down[0].astype(BF16), row(ln2_g[0]), row(ln2_b[0]))

    h3 = h.reshape(bsz, seq, d)
    k2, vt2, kmean = _proj_kv(h3, w_kv_shared[:, 0:D_MODEL].astype(BF16),
                              w_kv_shared[:, D_MODEL:].T.astype(BF16))
    qt2, mask = _proj_q_gate(h3, w_q_b[0].T.astype(BF16), kmean)
    o = _moba_attn(qt2, k2, vt2, bias, mask)
    h = _outproj_ln(o.reshape(m, d), h, w_out_b[0].astype(BF16), row(ln1_g[1]), row(ln1_b[1]))
    h = _mlp_ln(h, w_up[1].astype(BF16), w_down[1].astype(BF16), row(ln2_g[1]), row(ln2_b[1]))
    return h.reshape(bsz, seq, d)
```

```python
import functools
import math

import jax
import jax.numpy as jnp
from jax import lax
from jax.experimental import pallas as pl
from jax.experimental.pallas import tpu as pltpu

F32 = jnp.float32
BF16 = jnp.bfloat16

D_MODEL = 1024
D_FF = 4 * D_MODEL
DEPTH = 2
HEAD_DIM = 64
PAIR = 2 * HEAD_DIM
N_PAIRS = D_MODEL // PAIR
T = 256
TQS = 2 * T
N_SUB = 4
TQ = N_SUB * T
N_QT = TQ // TQS
COLS = 2 * TQ
N_CHUNKS = 2 * N_SUB
LANES = 128
COLS_PADDED = COLS + 2 * LANES
BF16_ROWS = 16
MOBA_TOPK = 3
REL_BUCKETS = 32
REL_MAX_DIST = 128
LN_EPS = 1e-5
ALPHA = (2 * DEPTH) ** 0.25
LAM_INIT_A = 0.8 - 0.6 * math.exp(-0.3 * 0)
LOG2E = 1.0 / math.log(2.0)
Q_SCALE = HEAD_DIM ** -0.5 * LOG2E
NEG = -1e30
V7X_VMEM_LIMIT = 56 * 1024 * 1024
TM = TQS
DV_A = PAIR
VT_ROWS_A = DV_A + BF16_ROWS
VT_ROWS_B = HEAD_DIM + BF16_ROWS


def _params(n_axes, flags=None):
    return pltpu.CompilerParams(
        dimension_semantics=("arbitrary",) * n_axes,
        vmem_limit_bytes=V7X_VMEM_LIMIT,
        flags=flags,
    )


def _nt_dot(a, b):
    return lax.dot_general(a, b, (((1,), (1,)), ((), ())), preferred_element_type=F32)


def _split_dot(a, b):
    a_hi = a.astype(BF16)
    a_lo = (a - a_hi.astype(F32)).astype(BF16)
    b_hi = b.astype(BF16)
    b_lo = (b - b_hi.astype(F32)).astype(BF16)
    dot = functools.partial(jnp.dot, preferred_element_type=F32)
    return dot(a_hi, b_hi) + (dot(a_hi, b_lo) + dot(a_lo, b_hi))


def _layer_norm(y, g, b):
    mu = jnp.mean(y, axis=-1, keepdims=True)
    yc = y - mu
    var = jnp.mean(yc * yc, axis=-1, keepdims=True)
    return yc * lax.rsqrt(var + LN_EPS) * g + b


def _chunk(c):
    return slice(c * T, (c + 1) * T)


def _bias_tile_kernel(tab_ref, out_ref):
    pr = pl.program_id(0)
    j = lax.broadcasted_iota(jnp.int32, (T, T), 0)
    i = lax.broadcasted_iota(jnp.int32, (T, T), 1)
    max_exact = REL_BUCKETS // 2
    for o in range(2):
        d = i - j + o * T
        n = jnp.maximum(d, 0)
        nf = jnp.maximum(n, max_exact).astype(F32)
        large = max_exact + (jnp.log(nf / max_exact) / math.log(REL_MAX_DIST / max_exact)
                             * (REL_BUCKETS - max_exact)).astype(jnp.int32)
        large = jnp.minimum(large, REL_BUCKETS - 1)
        bucket = jnp.where(n < max_exact, n, large)
        for m in range(2):
            col = 2 * pr + m
            far = tab_ref[REL_BUCKETS - 1, col]
            val = jnp.zeros((T, T), F32)
            for b in range(REL_BUCKETS - 1):
                val = jnp.where(bucket == b, (tab_ref[b, col] - far) * LOG2E, val)
            if o == 0:
                val = jnp.where(d >= 0, val, NEG)
            out_ref[0, o, m] = val


def _bias_tiles(rel_bias):
    return pl.pallas_call(
        _bias_tile_kernel,
        grid=(N_PAIRS,),
        in_specs=[pl.BlockSpec(memory_space=pltpu.SMEM)],
        out_specs=pl.BlockSpec((1, 2, 2, T, T), lambda p: (p, 0, 0, 0, 0)),
        out_shape=jax.ShapeDtypeStruct((N_PAIRS, 2, 2, T, T), F32),
        compiler_params=_params(1),
        name="bias_tiles",
    )(rel_bias)


def _store_q_tile(ref, qt):
    for p in range(N_PAIRS):
        ref[0, p, 0] = qt[p * PAIR:(p + 1) * PAIR].astype(BF16)


def _store_k_blocks(ref, k):
    for t in range(TM // T):
        for p in range(N_PAIRS):
            ref[0, p, t] = k[t * T:(t + 1) * T, p * PAIR:(p + 1) * PAIR].astype(BF16)


def _store_vt_blocks(ref, vt, width):
    groups = PAIR // width
    ones = jnp.ones((BF16_ROWS, T), BF16)
    for t in range(TM // T):
        for p in range(N_PAIRS):
            for g in range(groups):
                r0 = g * (width + BF16_ROWS)
                rows = slice(p * PAIR + g * width, p * PAIR + (g + 1) * width)
                ref[0, p, t, r0:r0 + width] = vt[rows, t * T:(t + 1) * T].astype(BF16)
                ref[0, p, t, r0 + width:r0 + width + BF16_ROWS] = ones


def _proj_a_kernel(x_ref, wqt_ref, wk_ref, wvt_ref, qt_ref, k_ref, vt_ref):
    xb = x_ref[0].astype(BF16)
    _store_q_tile(qt_ref, _nt_dot(wqt_ref[...], xb) * Q_SCALE)
    _store_k_blocks(k_ref, jnp.dot(xb, wk_ref[...], preferred_element_type=F32))
    _store_vt_blocks(vt_ref, _nt_dot(wvt_ref[...], xb), DV_A)


def _q_tile_spec():
    return pl.BlockSpec((1, N_PAIRS, 1, PAIR, TQS), lambda b, i: (b, 0, i, 0, 0))


def _k_blocks_spec():
    return pl.BlockSpec((1, N_PAIRS, TM // T, T, PAIR), lambda b, i: (b, 0, i, 0, 0))


def _vt_blocks_spec(rows):
    return pl.BlockSpec((1, N_PAIRS, TM // T, rows, T), lambda b, i: (b, 0, i, 0, 0))


def _weight_spec(shape):
    return pl.BlockSpec(shape, lambda *_: (0,) * len(shape), pipeline_mode=pl.Buffered(1))


def _proj_a(x, wqt, wk, wvt):
    bsz, seq, _ = x.shape
    nb = seq // T
    return pl.pallas_call(
        _proj_a_kernel,
        grid=(bsz, seq // TM),
        in_specs=[
            pl.BlockSpec((1, TM, D_MODEL), lambda b, i: (b, i, 0)),
            _weight_spec((D_MODEL, D_MODEL)),
            _weight_spec((D_MODEL, D_MODEL)),
            _weight_spec((D_MODEL, D_MODEL)),
        ],
        out_specs=[_q_tile_spec(), _k_blocks_spec(), _vt_blocks_spec(VT_ROWS_A)],
        out_shape=[
            jax.ShapeDtypeStruct((bsz, N_PAIRS, seq // TQS, PAIR, TQS), BF16),
            jax.ShapeDtypeStruct((bsz, N_PAIRS, nb, T, PAIR), BF16),
            jax.ShapeDtypeStruct((bsz, N_PAIRS, nb, VT_ROWS_A, T), BF16),
        ],
        compiler_params=_params(2),
        name="proj_a",
    )(x, wqt, wk, wvt)


def _proj_kv_kernel(h_ref, wk_ref, wvt_ref, k_ref, vt_ref, kmean_ref):
    hb = h_ref[0].astype(BF16)
    k = jnp.dot(hb, wk_ref[...], preferred_element_type=F32)
    _store_k_blocks(k_ref, k)
    for t in range(TM // T):
        blk = pl.program_id(1) * (TM // T) + t
        kmean_ref[0, pl.ds(blk, 1), :] = jnp.mean(k[t * T:(t + 1) * T], axis=0, keepdims=True)
    _store_vt_blocks(vt_ref, _nt_dot(wvt_ref[...], hb), HEAD_DIM)


def _proj_kv(h, wk, wvt):
    bsz, seq, _ = h.shape
    nb = seq // T
    return pl.pallas_call(
        _proj_kv_kernel,
        grid=(bsz, seq // TM),
        in_specs=[
            pl.BlockSpec((1, TM, D_MODEL), lambda b, i: (b, i, 0)),
            _weight_spec((D_MODEL, D_MODEL)),
            _weight_spec((D_MODEL, D_MODEL)),
        ],
        out_specs=[
            _k_blocks_spec(),
            _vt_blocks_spec(2 * VT_ROWS_B),
            pl.BlockSpec((1, nb, D_MODEL), lambda b, i: (b, 0, 0)),
        ],
        out_shape=[
            jax.ShapeDtypeStruct((bsz, N_PAIRS, nb, T, PAIR), BF16),
            jax.ShapeDtypeStruct((bsz, N_PAIRS, nb, 2 * VT_ROWS_B, T), BF16),
            jax.ShapeDtypeStruct((bsz, nb, D_MODEL), F32),
        ],
        compiler_params=_params(2),
        name="proj_kv",
    )(h, wk, wvt)


def _proj_q_gate_kernel(h_ref, wqt_ref, kmean_ref, qt_ref, mask_ref, *, n_blocks):
    hb = h_ref[0].astype(BF16)
    qt = _nt_dot(wqt_ref[...], hb)
    _store_q_tile(qt_ref, qt * Q_SCALE)

    km = kmean_ref[0]
    lane = lax.broadcasted_iota(jnp.int32, (n_blocks, PAIR), 1)
    blk = lax.broadcasted_iota(jnp.int32, (n_blocks, TM), 0)
    col = lax.broadcasted_iota(jnp.int32, (1, TM), 1)
    q_blk = pl.program_id(1) * (TM // T) + lax.shift_right_logical(col, T.bit_length() - 1)
    past = blk < q_blk
    blk_f = blk.astype(F32)
    for p in range(N_PAIRS):
        km_p = km[:, p * PAIR:(p + 1) * PAIR]
        km2 = jnp.concatenate(
            [jnp.where(lane < HEAD_DIM, km_p, 0.0), jnp.where(lane >= HEAD_DIM, km_p, 0.0)],
            axis=0)
        g2 = _split_dot(km2, qt[p * PAIR:(p + 1) * PAIR])
        for sub in range(2):
            g = jnp.where(past, g2[sub * n_blocks:(sub + 1) * n_blocks], -jnp.inf)
            keep = jnp.zeros((n_blocks, TM), F32)
            for _ in range(MOBA_TOPK):
                top = jnp.max(g, axis=0, keepdims=True)
                first = jnp.min(jnp.where(g == top, blk_f, float(n_blocks)), axis=0, keepdims=True)
                pick = (blk_f == first) & (top > -jnp.inf)
                keep = jnp.where(pick, 1.0, keep)
                g = jnp.where(pick, -jnp.inf, g)
            mask_ref[0, p, 0, :, sub * TQS:(sub + 1) * TQS] = jnp.where(keep > 0.0, 0.0, NEG)


def _proj_q_gate(h, wqt, kmean):
    bsz, seq, _ = h.shape
    nb = seq // T
    return pl.pallas_call(
        functools.partial(_proj_q_gate_kernel, n_blocks=nb),
        grid=(bsz, seq // TM),
        in_specs=[
            pl.BlockSpec((1, TM, D_MODEL), lambda b, i: (b, i, 0)),
            _weight_spec((D_MODEL, D_MODEL)),
            pl.BlockSpec((1, nb, D_MODEL), lambda b, i: (b, 0, 0)),
        ],
        out_specs=[
            _q_tile_spec(),
            pl.BlockSpec((1, N_PAIRS, 1, nb, 2 * TQS), lambda b, i: (b, 0, i, 0, 0)),
        ],
        out_shape=[
            jax.ShapeDtypeStruct((bsz, N_PAIRS, seq // TQS, PAIR, TQS), BF16),
            jax.ShapeDtypeStruct((bsz, N_PAIRS, seq // TQS, nb, 2 * TQS), F32),
        ],
        compiler_params=_params(2),
        name="proj_q_gate",
    )(h, wqt, kmean)


class _AttnRefs:
    def __init__(self, k_ref, vt_ref, q2, s, mb, p, al, m, acc):
        self.k, self.vt, self.q2 = k_ref, vt_ref, q2
        self.s, self.mb, self.p, self.al, self.m, self.acc = s, mb, p, al, m, acc


def _fill_q2(q2_ref, qt_ref):
    zeros = jnp.zeros((HEAD_DIM, TQ), BF16)
    q2_ref[0:HEAD_DIM, TQ:COLS] = zeros
    q2_ref[HEAD_DIM:PAIR, 0:TQ] = zeros
    for t in range(N_QT):
        qt = qt_ref[0, 0, t]
        q2_ref[0:HEAD_DIM, t * TQS:(t + 1) * TQS] = qt[0:HEAD_DIM]
        q2_ref[HEAD_DIM:PAIR, TQ + t * TQS:TQ + (t + 1) * TQS] = qt[HEAD_DIM:PAIR]


def _stage_a(r, par, blk, extras):
    kj = r.k[0, 0, blk]
    for c in range(N_CHUNKS):
        if isinstance(extras[c], str):
            continue
        s = jnp.dot(kj, r.q2[:, _chunk(c)], preferred_element_type=F32)
        if extras[c] is not None:
            s = s + extras[c]
        r.s[par][:, _chunk(c)] = s
        r.mb[par][:, _chunk(c)] = jnp.max(s, axis=0, keepdims=True)


def _stage_b(r, par, chunks):
    for c in chunks:
        m_old = r.m[:, _chunk(c)]
        m_new = jnp.maximum(m_old, r.mb[par][:, _chunk(c)])
        r.al[par][:, _chunk(c)] = jnp.exp2(m_old - m_new)
        r.m[:, _chunk(c)] = m_new
        r.p[par][:, _chunk(c)] = jnp.exp2(r.s[par][:, _chunk(c)] - m_new).astype(BF16)


def _stage_c(r, par, blk, vt_rows, chunks):
    vt = r.vt[0, 0, blk]
    for c in chunks:
        pv = jnp.dot(vt[vt_rows[c // N_SUB]], r.p[par][:, _chunk(c)], preferred_element_type=F32)
        r.acc[:, _chunk(c)] = r.al[par][:, _chunk(c)] * r.acc[:, _chunk(c)] + pv


ALL_CHUNKS = tuple(range(N_CHUNKS))


def _live_chunks(visit):
    if visit >= N_SUB:
        return ALL_CHUNKS
    return tuple(c for c in ALL_CHUNKS if c % N_SUB >= visit)


def _visit_extras(bias_ref, offset_of, mask_of):
    extras = []
    for c in range(N_CHUNKS):
        g, s = divmod(c, N_SUB)
        off = offset_of(s)
        if off < 0:
            extras.append("future")
            continue
        term = bias_ref[0, off, g] if off <= 1 else None
        mask = mask_of(c) if off >= 1 else None
        if term is None:
            extras.append(mask)
        else:
            extras.append(term if mask is None else term + mask)
    return extras


def _attend(r, tile, vt_rows, bias_ref, mask_row):
    r.m[...] = jnp.full(r.m.shape, NEG, F32)
    r.acc[...] = jnp.zeros(r.acc.shape, F32)
    base = N_SUB * tile
    far = 2

    def mask_of(blk):
        return (lambda c: None) if mask_row is None else (lambda c: mask_row(blk, c))

    def block_of(n):
        return jnp.where(n < N_SUB, base + n, jnp.where(n == N_SUB, base - 1, n - N_SUB - 1))

    def stage(n, blk, extras):
        _stage_a(r, n % 2, blk, extras)
        if n >= 1:
            _stage_b(r, (n - 1) % 2, _live_chunks(n - 1))
        if n >= 2:
            _stage_c(r, n % 2, base + n - 2, vt_rows, _live_chunks(n - 2))

    for d in range(N_SUB):
        stage(d, base + d, _visit_extras(bias_ref, lambda s: s - d, mask_of(base + d)))

    @pl.when(tile == 0)
    def _():
        _stage_b(r, 1, _live_chunks(N_SUB - 1))
        _stage_c(r, 0, base + N_SUB - 2, vt_rows, _live_chunks(N_SUB - 2))
        _stage_c(r, 1, base + N_SUB - 1, vt_rows, _live_chunks(N_SUB - 1))

    @pl.when(tile > 0)
    def _():
        prev = _visit_extras(bias_ref, lambda s: 1 if s == 0 else far, mask_of(base - 1))
        stage(N_SUB, base - 1, prev)
        stage(N_SUB + 1, 0, _visit_extras(bias_ref, lambda s: far, mask_of(0)))

        def body(t, carry):
            n = N_SUB + 2 + 2 * t
            first = 1 + 2 * t
            _stage_a(r, 0, first, _visit_extras(bias_ref, lambda s: far, mask_of(first)))
            _stage_b(r, 1, ALL_CHUNKS)
            _stage_c(r, 0, block_of(n - 2), vt_rows, ALL_CHUNKS)
            _stage_a(r, 1, first + 1, _visit_extras(bias_ref, lambda s: far, mask_of(first + 1)))
            _stage_b(r, 0, ALL_CHUNKS)
            _stage_c(r, 1, first - 1, vt_rows, ALL_CHUNKS)
            return carry

        lax.fori_loop(0, base // 2 - 1, body, 0)

        n_visits = base + N_SUB
        _stage_b(r, 1, ALL_CHUNKS)
        _stage_c(r, 0, block_of(n_visits - 2), vt_rows, ALL_CHUNKS)
        _stage_c(r, 1, block_of(n_visits - 1), vt_rows, ALL_CHUNKS)


def _attn_scratch(acc_rows):
    return [
        pltpu.VMEM((PAIR, COLS_PADDED), BF16),
        pltpu.VMEM((T, COLS_PADDED), F32),
        pltpu.VMEM((T, COLS_PADDED), F32),
        pltpu.VMEM((1, COLS), F32),
        pltpu.VMEM((1, COLS), F32),
        pltpu.VMEM((T, COLS_PADDED), BF16),
        pltpu.VMEM((T, COLS_PADDED), BF16),
        pltpu.VMEM((1, COLS), F32),
        pltpu.VMEM((1, COLS), F32),
        pltpu.VMEM((1, COLS), F32),
        pltpu.VMEM((acc_rows, COLS_PADDED), F32),
    ]


def _make_refs(k_ref, vt_ref, scratch):
    q2, s0, s1, mb0, mb1, p0, p1, al0, al1, m, acc = scratch
    return _AttnRefs(k_ref, vt_ref, q2, (s0, s1), (mb0, mb1), (p0, p1), (al0, al1), m, acc)


def _attn_in_specs(nb, vt_rows):
    return [
        pl.BlockSpec((1, 1, N_QT, PAIR, TQS), lambda b, p, i: (b, p, i, 0, 0)),
        pl.BlockSpec((1, 1, nb, T, PAIR), lambda b, p, i: (b, p, 0, 0, 0)),
        pl.BlockSpec((1, 1, nb, vt_rows, T), lambda b, p, i: (b, p, 0, 0, 0)),
        pl.BlockSpec((1, 2, 2, T, T), lambda b, p, i: (p, 0, 0, 0, 0)),
    ]


def _diff_attn_kernel(qt_ref, k_ref, vt_ref, bias_ref, lam_ref, subln_ref, o_ref, *scratch):
    r = _make_refs(k_ref, vt_ref, scratch)
    _fill_q2(r.q2, qt_ref)
    _attend(r, pl.program_id(2), [slice(0, VT_ROWS_A)] * 2, bias_ref, None)

    lp = lam_ref[...]
    lam = (jnp.exp(jnp.sum(lp[0:1] * lp[1:2], axis=1, keepdims=True))
           - jnp.exp(jnp.sum(lp[2:3] * lp[3:4], axis=1, keepdims=True)) + LAM_INIT_A)
    o_all = r.acc[0:DV_A, 0:COLS] * (1.0 / r.acc[DV_A:DV_A + 1, 0:COLS])
    o = o_all[:, 0:TQ] - lam * o_all[:, TQ:COLS]
    o = o * lax.rsqrt(jnp.mean(o * o, axis=0, keepdims=True) + LN_EPS)
    o = o * subln_ref[...] * (1.0 - LAM_INIT_A)
    o_ref[0] = o.T.astype(BF16)


def _diff_attn(qt, k, vt, bias, lam_p, subln_col):
    bsz, _, nqs, _, _ = qt.shape
    nb = k.shape[2]
    return pl.pallas_call(
        _diff_attn_kernel,
        grid=(bsz, N_PAIRS, nqs // N_QT),
        in_specs=_attn_in_specs(nb, VT_ROWS_A) + [
            pl.BlockSpec((4, HEAD_DIM), lambda b, p, i: (0, 0)),
            pl.BlockSpec((PAIR, 1), lambda b, p, i: (0, 0)),
        ],
        out_specs=pl.BlockSpec((1, TQ, PAIR), lambda b, p, i: (b, i, p)),
        out_shape=jax.ShapeDtypeStruct((bsz, nqs * TQS, D_MODEL), BF16),
        scratch_shapes=_attn_scratch(VT_ROWS_A),
        compiler_params=_params(3),
        name="diff_attn",
    )(qt, k, vt, bias, lam_p, subln_col)


def _moba_attn_kernel(qt_ref, k_ref, vt_ref, bias_ref, mask_ref, o_ref, *scratch):
    r = _make_refs(k_ref, vt_ref, scratch)
    _fill_q2(r.q2, qt_ref)

    def mask_row(blk, c):
        g, s = divmod(c, N_SUB)
        t, s_in = divmod(s, TQS // T)
        col0 = g * TQS + s_in * T
        return mask_ref[0, 0, t, pl.ds(blk, 1), col0:col0 + T]

    rows = [slice(0, VT_ROWS_B), slice(VT_ROWS_B, 2 * VT_ROWS_B)]
    _attend(r, pl.program_id(2), rows, bias_ref, mask_row)

    o_all = r.acc[0:HEAD_DIM, 0:COLS] * (1.0 / r.acc[HEAD_DIM:HEAD_DIM + 1, 0:COLS])
    o = jnp.concatenate([o_all[:, 0:TQ], o_all[:, TQ:COLS]], axis=0)
    o_ref[0] = o.T.astype(BF16)


def _moba_attn(qt, k, vt, bias, mask):
    bsz, _, nqs, _, _ = qt.shape
    nb = k.shape[2]
    return pl.pallas_call(
        _moba_attn_kernel,
        grid=(bsz, N_PAIRS, nqs // N_QT),
        in_specs=_attn_in_specs(nb, 2 * VT_ROWS_B) + [
            pl.BlockSpec((1, 1, N_QT, nb, 2 * TQS), lambda b, p, i: (b, p, i, 0, 0)),
        ],
        out_specs=pl.BlockSpec((1, TQ, PAIR), lambda b, p, i: (b, i, p)),
        out_shape=jax.ShapeDtypeStruct((bsz, nqs * TQS, D_MODEL), BF16),
        scratch_shapes=_attn_scratch(VT_ROWS_B),
        compiler_params=_params(3),
        name="moba_attn",
    )(qt, k, vt, bias, mask)


def _post_attn_kernel(o_ref, h_ref, wo_ref, g1_ref, b1_ref, wup_ref, wdn_ref, g2_ref, b2_ref,
                      out_ref):
    y = ALPHA * h_ref[...] + jnp.dot(o_ref[...], wo_ref[...], preferred_element_type=F32)
    h = _layer_norm(y, g1_ref[...], b1_ref[...])
    hb = h.astype(BF16)
    y = ALPHA * h
    for c in range(D_FF // D_MODEL):
        cols = slice(c * D_MODEL, (c + 1) * D_MODEL)
        u = jnp.dot(hb, wup_ref[:, cols], preferred_element_type=F32)
        u = jnp.square(jnp.maximum(u, 0.0)).astype(BF16)
        y = y + jnp.dot(u, wdn_ref[cols, :], preferred_element_type=F32)
    out_ref[...] = _layer_norm(y, g2_ref[...], b2_ref[...])


def _post_attn(o, h, wo, g1, b1, wup, wdn, g2, b2):
    m = h.shape[0]
    tile = pl.BlockSpec((TM, D_MODEL), lambda i: (i, 0))
    return pl.pallas_call(
        _post_attn_kernel,
        grid=(m // TM,),
        in_specs=[
            tile,
            tile,
            _weight_spec((D_MODEL, D_MODEL)),
            _weight_spec((1, D_MODEL)),
            _weight_spec((1, D_MODEL)),
            _weight_spec((D_MODEL, D_FF)),
            _weight_spec((D_FF, D_MODEL)),
            _weight_spec((1, D_MODEL)),
            _weight_spec((1, D_MODEL)),
        ],
        out_specs=tile,
        out_shape=jax.ShapeDtypeStruct((m, D_MODEL), F32),
        compiler_params=_params(1),
        name="post_attn",
    )(o, h, wo, g1, b1, wup, wdn, g2, b2)


def kernel(x, w_in_a, lam_a, subln_a, w_out_a, w_kv_shared, w_q_b, w_out_b, rel_bias,
           ln1_g, ln1_b, ln2_g, ln2_b, w_up, w_down):
    bsz, seq, d = x.shape
    assert d == D_MODEL and seq % TM == 0 and TM % T == 0
    assert w_in_a.shape[0] == 1 and w_q_b.shape[0] == 1 and w_up.shape[0] == DEPTH
    m = bsz * seq
    row = lambda v: v.reshape(1, D_MODEL)

    bias = _bias_tiles(rel_bias)

    w_in = w_in_a[0]
    wqt = w_in[:, 0:D_MODEL].T.astype(BF16)
    wk = w_in[:, D_MODEL:2 * D_MODEL].astype(BF16)
    wvt = w_in[:, 2 * D_MODEL:].T.astype(BF16)
    qt, k, vt = _proj_a(x, wqt, wk, wvt)
    o = _diff_attn(qt, k, vt, bias, lam_a[0], subln_a[0].reshape(PAIR, 1))
    h = _post_attn(o.reshape(m, d), x.reshape(m, d), w_out_a[0].astype(BF16),
                   row(ln1_g[0]), row(ln1_b[0]), w_up[0].astype(BF16), w_down[0].astype(BF16),
                   row(ln2_g[0]), row(ln2_b[0]))

    h3 = h.reshape(bsz, seq, d)
    k2, vt2, kmean = _proj_kv(h3, w_kv_shared[:, 0:D_MODEL].astype(BF16),
                              w_kv_shared[:, D_MODEL:].T.astype(BF16))
    qt2, mask = _proj_q_gate(h3, w_q_b[0].T.astype(BF16), kmean)
    o = _moba_attn(qt2, k2, vt2, bias, mask)
    h = _post_attn(o.reshape(m, d), h, w_out_b[0].astype(BF16), row(ln1_g[1]), row(ln1_b[1]),
                   w_up[1].astype(BF16), w_down[1].astype(BF16), row(ln2_g[1]), row(ln2_b[1]))
    return h.reshape(bsz, seq, d)
```

```python
import functools
import math

import jax
import jax.numpy as jnp
from jax import lax
from jax.experimental import pallas as pl
from jax.experimental.pallas import tpu as pltpu

F32 = jnp.float32
BF16 = jnp.bfloat16

D_MODEL = 1024
D_FF = 4 * D_MODEL
DEPTH = 2
HEAD_DIM = 64
PAIR = 2 * HEAD_DIM
N_PAIRS = D_MODEL // PAIR
T = 256
TQS = 2 * T
N_SUB = 8
TQ = N_SUB * T
N_QT = TQ // TQS
COLS = 2 * TQ
N_CHUNKS = 2 * N_SUB
LANES = 128
COLS_PADDED = COLS + 0 * LANES
BF16_ROWS = 16
MOBA_TOPK = 3
REL_BUCKETS = 32
REL_MAX_DIST = 128
LN_EPS = 1e-5
ALPHA = (2 * DEPTH) ** 0.25
LAM_INIT_A = 0.8 - 0.6 * math.exp(-0.3 * 0)
LOG2E = 1.0 / math.log(2.0)
Q_SCALE = HEAD_DIM ** -0.5 * LOG2E
NEG = -1e30
V7X_VMEM_LIMIT = 56 * 1024 * 1024
TM = TQS
DV_A = PAIR
VT_ROWS_A = DV_A + BF16_ROWS
VT_ROWS_B = HEAD_DIM + BF16_ROWS


def _params(n_axes, flags=None):
    return pltpu.CompilerParams(
        dimension_semantics=("arbitrary",) * n_axes,
        vmem_limit_bytes=V7X_VMEM_LIMIT,
        flags=flags,
    )


def _nt_dot(a, b):
    return lax.dot_general(a, b, (((1,), (1,)), ((), ())), preferred_element_type=F32)


def _split_dot(a, b):
    a_hi = a.astype(BF16)
    a_lo = (a - a_hi.astype(F32)).astype(BF16)
    b_hi = b.astype(BF16)
    b_lo = (b - b_hi.astype(F32)).astype(BF16)
    dot = functools.partial(jnp.dot, preferred_element_type=F32)
    return dot(a_hi, b_hi) + (dot(a_hi, b_lo) + dot(a_lo, b_hi))


def _layer_norm(y, g, b):
    mu = jnp.mean(y, axis=-1, keepdims=True)
    yc = y - mu
    var = jnp.mean(yc * yc, axis=-1, keepdims=True)
    return yc * lax.rsqrt(var + LN_EPS) * g + b


def _chunk(c):
    return slice(c * T, (c + 1) * T)


def _bias_tile_kernel(tab_ref, out_ref):
    pr = pl.program_id(0)
    j = lax.broadcasted_iota(jnp.int32, (T, T), 0)
    i = lax.broadcasted_iota(jnp.int32, (T, T), 1)
    max_exact = REL_BUCKETS // 2
    for o in range(2):
        d = i - j + o * T
        n = jnp.maximum(d, 0)
        nf = jnp.maximum(n, max_exact).astype(F32)
        large = max_exact + (jnp.log(nf / max_exact) / math.log(REL_MAX_DIST / max_exact)
                             * (REL_BUCKETS - max_exact)).astype(jnp.int32)
        large = jnp.minimum(large, REL_BUCKETS - 1)
        bucket = jnp.where(n < max_exact, n, large)
        for m in range(2):
            col = 2 * pr + m
            far = tab_ref[REL_BUCKETS - 1, col]
            val = jnp.zeros((T, T), F32)
            for b in range(REL_BUCKETS - 1):
                val = jnp.where(bucket == b, (tab_ref[b, col] - far) * LOG2E, val)
            if o == 0:
                val = jnp.where(d >= 0, val, NEG)
            out_ref[0, o, m] = val


def _bias_tiles(rel_bias):
    return pl.pallas_call(
        _bias_tile_kernel,
        grid=(N_PAIRS,),
        in_specs=[pl.BlockSpec(memory_space=pltpu.SMEM)],
        out_specs=pl.BlockSpec((1, 2, 2, T, T), lambda p: (p, 0, 0, 0, 0)),
        out_shape=jax.ShapeDtypeStruct((N_PAIRS, 2, 2, T, T), F32),
        compiler_params=_params(1),
        name="bias_tiles",
    )(rel_bias)


def _store_q_tile(ref, qt):
    for p in range(N_PAIRS):
        ref[0, p, 0] = qt[p * PAIR:(p + 1) * PAIR].astype(BF16)


def _store_k_blocks(ref, k):
    for t in range(TM // T):
        for p in range(N_PAIRS):
            ref[0, p, t] = k[t * T:(t + 1) * T, p * PAIR:(p + 1) * PAIR].astype(BF16)


def _store_vt_blocks(ref, vt, width):
    groups = PAIR // width
    ones = jnp.ones((BF16_ROWS, T), BF16)
    for t in range(TM // T):
        for p in range(N_PAIRS):
            for g in range(groups):
                r0 = g * (width + BF16_ROWS)
                rows = slice(p * PAIR + g * width, p * PAIR + (g + 1) * width)
                ref[0, p, t, r0:r0 + width] = vt[rows, t * T:(t + 1) * T].astype(BF16)
                ref[0, p, t, r0 + width:r0 + width + BF16_ROWS] = ones


def _proj_a_kernel(x_ref, wqt_ref, wk_ref, wvt_ref, qt_ref, k_ref, vt_ref):
    xb = x_ref[0].astype(BF16)
    _store_q_tile(qt_ref, _nt_dot(wqt_ref[...], xb) * Q_SCALE)
    _store_k_blocks(k_ref, jnp.dot(xb, wk_ref[...], preferred_element_type=F32))
    _store_vt_blocks(vt_ref, _nt_dot(wvt_ref[...], xb), DV_A)


def _q_tile_spec():
    return pl.BlockSpec((1, N_PAIRS, 1, PAIR, TQS), lambda b, i: (b, 0, i, 0, 0))


def _k_blocks_spec():
    return pl.BlockSpec((1, N_PAIRS, TM // T, T, PAIR), lambda b, i: (b, 0, i, 0, 0))


def _vt_blocks_spec(rows):
    return pl.BlockSpec((1, N_PAIRS, TM // T, rows, T), lambda b, i: (b, 0, i, 0, 0))


def _weight_spec(shape):
    return pl.BlockSpec(shape, lambda *_: (0,) * len(shape), pipeline_mode=pl.Buffered(1))


def _proj_a(x, wqt, wk, wvt):
    bsz, seq, _ = x.shape
    nb = seq // T
    return pl.pallas_call(
        _proj_a_kernel,
        grid=(bsz, seq // TM),
        in_specs=[
            pl.BlockSpec((1, TM, D_MODEL), lambda b, i: (b, i, 0)),
            _weight_spec((D_MODEL, D_MODEL)),
            _weight_spec((D_MODEL, D_MODEL)),
            _weight_spec((D_MODEL, D_MODEL)),
        ],
        out_specs=[_q_tile_spec(), _k_blocks_spec(), _vt_blocks_spec(VT_ROWS_A)],
        out_shape=[
            jax.ShapeDtypeStruct((bsz, N_PAIRS, seq // TQS, PAIR, TQS), BF16),
            jax.ShapeDtypeStruct((bsz, N_PAIRS, nb, T, PAIR), BF16),
            jax.ShapeDtypeStruct((bsz, N_PAIRS, nb, VT_ROWS_A, T), BF16),
        ],
        compiler_params=_params(2),
        name="proj_a",
    )(x, wqt, wk, wvt)


def _proj_kv_kernel(h_ref, wk_ref, wvt_ref, k_ref, vt_ref, kmean_ref):
    hb = h_ref[0].astype(BF16)
    k = jnp.dot(hb, wk_ref[...], preferred_element_type=F32)
    _store_k_blocks(k_ref, k)
    for t in range(TM // T):
        blk = pl.program_id(1) * (TM // T) + t
        kmean_ref[0, pl.ds(blk, 1), :] = jnp.mean(k[t * T:(t + 1) * T], axis=0, keepdims=True)
    _store_vt_blocks(vt_ref, _nt_dot(wvt_ref[...], hb), HEAD_DIM)


def _proj_kv(h, wk, wvt):
    bsz, seq, _ = h.shape
    nb = seq // T
    return pl.pallas_call(
        _proj_kv_kernel,
        grid=(bsz, seq // TM),
        in_specs=[
            pl.BlockSpec((1, TM, D_MODEL), lambda b, i: (b, i, 0)),
            _weight_spec((D_MODEL, D_MODEL)),
            _weight_spec((D_MODEL, D_MODEL)),
        ],
        out_specs=[
            _k_blocks_spec(),
            _vt_blocks_spec(2 * VT_ROWS_B),
            pl.BlockSpec((1, nb, D_MODEL), lambda b, i: (b, 0, 0)),
        ],
        out_shape=[
            jax.ShapeDtypeStruct((bsz, N_PAIRS, nb, T, PAIR), BF16),
            jax.ShapeDtypeStruct((bsz, N_PAIRS, nb, 2 * VT_ROWS_B, T), BF16),
            jax.ShapeDtypeStruct((bsz, nb, D_MODEL), F32),
        ],
        compiler_params=_params(2),
        name="proj_kv",
    )(h, wk, wvt)


def _proj_q_gate_kernel(h_ref, wqt_ref, kmean_ref, qt_ref, mask_ref, *, n_blocks):
    hb = h_ref[0].astype(BF16)
    qt = _nt_dot(wqt_ref[...], hb)
    _store_q_tile(qt_ref, qt * Q_SCALE)

    km = kmean_ref[0]
    lane = lax.broadcasted_iota(jnp.int32, (n_blocks, PAIR), 1)
    blk = lax.broadcasted_iota(jnp.int32, (n_blocks, TM), 0)
    col = lax.broadcasted_iota(jnp.int32, (1, TM), 1)
    q_blk = pl.program_id(1) * (TM // T) + lax.shift_right_logical(col, T.bit_length() - 1)
    past = blk < q_blk
    blk_f = blk.astype(F32)
    for p in range(N_PAIRS):
        km_p = km[:, p * PAIR:(p + 1) * PAIR]
        km2 = jnp.concatenate(
            [jnp.where(lane < HEAD_DIM, km_p, 0.0), jnp.where(lane >= HEAD_DIM, km_p, 0.0)],
            axis=0)
        g2 = _split_dot(km2, qt[p * PAIR:(p + 1) * PAIR])
        for sub in range(2):
            g = jnp.where(past, g2[sub * n_blocks:(sub + 1) * n_blocks], -jnp.inf)
            keep = jnp.zeros((n_blocks, TM), F32)
            for _ in range(MOBA_TOPK):
                top = jnp.max(g, axis=0, keepdims=True)
                first = jnp.min(jnp.where(g == top, blk_f, float(n_blocks)), axis=0, keepdims=True)
                pick = (blk_f == first) & (top > -jnp.inf)
                keep = jnp.where(pick, 1.0, keep)
                g = jnp.where(pick, -jnp.inf, g)
            mask_ref[0, p, 0, :, sub * TQS:(sub + 1) * TQS] = jnp.where(keep > 0.0, 0.0, NEG)


def _proj_q_gate(h, wqt, kmean):
    bsz, seq, _ = h.shape
    nb = seq // T
    return pl.pallas_call(
        functools.partial(_proj_q_gate_kernel, n_blocks=nb),
        grid=(bsz, seq // TM),
        in_specs=[
            pl.BlockSpec((1, TM, D_MODEL), lambda b, i: (b, i, 0)),
            _weight_spec((D_MODEL, D_MODEL)),
            pl.BlockSpec((1, nb, D_MODEL), lambda b, i: (b, 0, 0)),
        ],
        out_specs=[
            _q_tile_spec(),
            pl.BlockSpec((1, N_PAIRS, 1, nb, 2 * TQS), lambda b, i: (b, 0, i, 0, 0)),
        ],
        out_shape=[
            jax.ShapeDtypeStruct((bsz, N_PAIRS, seq // TQS, PAIR, TQS), BF16),
            jax.ShapeDtypeStruct((bsz, N_PAIRS, seq // TQS, nb, 2 * TQS), F32),
        ],
        compiler_params=_params(2),
        name="proj_q_gate",
    )(h, wqt, kmean)


class _AttnRefs:
    def __init__(self, k_ref, vt_ref, q2, s, mb, p, al, m, acc):
        self.k, self.vt, self.q2 = k_ref, vt_ref, q2
        self.s, self.mb, self.p, self.al, self.m, self.acc = s, mb, p, al, m, acc


def _fill_q2(q2_ref, qt_ref):
    zeros = jnp.zeros((HEAD_DIM, TQ), BF16)
    q2_ref[0:HEAD_DIM, TQ:COLS] = zeros
    q2_ref[HEAD_DIM:PAIR, 0:TQ] = zeros
    for t in range(N_QT):
        qt = qt_ref[0, 0, t]
        q2_ref[0:HEAD_DIM, t * TQS:(t + 1) * TQS] = qt[0:HEAD_DIM]
        q2_ref[HEAD_DIM:PAIR, TQ + t * TQS:TQ + (t + 1) * TQS] = qt[HEAD_DIM:PAIR]


def _stage_a(r, par, blk, extras):
    kj = r.k[0, 0, blk]
    for c in range(N_CHUNKS):
        if isinstance(extras[c], str):
            continue
        s = jnp.dot(kj, r.q2[:, _chunk(c)], preferred_element_type=F32)
        if extras[c] is not None:
            s = s + extras[c]
        r.s[par][:, _chunk(c)] = s
        r.mb[par][:, _chunk(c)] = jnp.max(s, axis=0, keepdims=True)


def _stage_b(r, par, chunks):
    for c in chunks:
        m_old = r.m[:, _chunk(c)]
        m_new = jnp.maximum(m_old, r.mb[par][:, _chunk(c)])
        r.al[par][:, _chunk(c)] = jnp.exp2(m_old - m_new)
        r.m[:, _chunk(c)] = m_new
        r.p[par][:, _chunk(c)] = jnp.exp2(r.s[par][:, _chunk(c)] - m_new).astype(BF16)


def _stage_c(r, par, blk, vt_rows, chunks):
    vt = r.vt[0, 0, blk]
    for c in chunks:
        pv = jnp.dot(vt[vt_rows[c // N_SUB]], r.p[par][:, _chunk(c)], preferred_element_type=F32)
        r.acc[:, _chunk(c)] = r.al[par][:, _chunk(c)] * r.acc[:, _chunk(c)] + pv


ALL_CHUNKS = tuple(range(N_CHUNKS))


def _live_chunks(visit):
    if visit >= N_SUB:
        return ALL_CHUNKS
    return tuple(c for c in ALL_CHUNKS if c % N_SUB >= visit)


def _visit_extras(bias_ref, offset_of, mask_of):
    extras = []
    for c in range(N_CHUNKS):
        g, s = divmod(c, N_SUB)
        off = offset_of(s)
        if off < 0:
            extras.append("future")
            continue
        term = bias_ref[0, off, g] if off <= 1 else None
        mask = mask_of(c) if off >= 1 else None
        if term is None:
            extras.append(mask)
        else:
            extras.append(term if mask is None else term + mask)
    return extras


def _attend(r, tile, vt_rows, bias_ref, mask_row):
    r.m[...] = jnp.full(r.m.shape, NEG, F32)
    r.acc[...] = jnp.zeros(r.acc.shape, F32)
    base = N_SUB * tile
    far = 2

    def mask_of(blk):
        return (lambda c: None) if mask_row is None else (lambda c: mask_row(blk, c))

    def block_of(n):
        return jnp.where(n < N_SUB, base + n, jnp.where(n == N_SUB, base - 1, n - N_SUB - 1))

    def stage(n, blk, extras):
        _stage_a(r, n % 2, blk, extras)
        if n >= 1:
            _stage_b(r, (n - 1) % 2, _live_chunks(n - 1))
        if n >= 2:
            _stage_c(r, n % 2, base + n - 2, vt_rows, _live_chunks(n - 2))

    for d in range(N_SUB):
        stage(d, base + d, _visit_extras(bias_ref, lambda s: s - d, mask_of(base + d)))

    @pl.when(tile == 0)
    def _():
        _stage_b(r, 1, _live_chunks(N_SUB - 1))
        _stage_c(r, 0, base + N_SUB - 2, vt_rows, _live_chunks(N_SUB - 2))
        _stage_c(r, 1, base + N_SUB - 1, vt_rows, _live_chunks(N_SUB - 1))

    @pl.when(tile > 0)
    def _():
        prev = _visit_extras(bias_ref, lambda s: 1 if s == 0 else far, mask_of(base - 1))
        stage(N_SUB, base - 1, prev)
        stage(N_SUB + 1, 0, _visit_extras(bias_ref, lambda s: far, mask_of(0)))

        def body(t, carry):
            n = N_SUB + 2 + 2 * t
            first = 1 + 2 * t
            _stage_a(r, 0, first, _visit_extras(bias_ref, lambda s: far, mask_of(first)))
            _stage_b(r, 1, ALL_CHUNKS)
            _stage_c(r, 0, block_of(n - 2), vt_rows, ALL_CHUNKS)
            _stage_a(r, 1, first + 1, _visit_extras(bias_ref, lambda s: far, mask_of(first + 1)))
            _stage_b(r, 0, ALL_CHUNKS)
            _stage_c(r, 1, first - 1, vt_rows, ALL_CHUNKS)
            return carry

        lax.fori_loop(0, base // 2 - 1, body, 0)

        n_visits = base + N_SUB
        _stage_b(r, 1, ALL_CHUNKS)
        _stage_c(r, 0, block_of(n_visits - 2), vt_rows, ALL_CHUNKS)
        _stage_c(r, 1, block_of(n_visits - 1), vt_rows, ALL_CHUNKS)


def _attn_scratch(acc_rows):
    return [
        pltpu.VMEM((PAIR, COLS_PADDED), BF16),
        pltpu.VMEM((T, COLS_PADDED), F32),
        pltpu.VMEM((T, COLS_PADDED), F32),
        pltpu.VMEM((1, COLS), F32),
        pltpu.VMEM((1, COLS), F32),
        pltpu.VMEM((T, COLS_PADDED), BF16),
        pltpu.VMEM((T, COLS_PADDED), BF16),
        pltpu.VMEM((1, COLS), F32),
        pltpu.VMEM((1, COLS), F32),
        pltpu.VMEM((1, COLS), F32),
        pltpu.VMEM((acc_rows, COLS_PADDED), F32),
    ]


def _make_refs(k_ref, vt_ref, scratch):
    q2, s0, s1, mb0, mb1, p0, p1, al0, al1, m, acc = scratch
    return _AttnRefs(k_ref, vt_ref, q2, (s0, s1), (mb0, mb1), (p0, p1), (al0, al1), m, acc)


def _attn_in_specs(nb, vt_rows):
    return [
        pl.BlockSpec((1, 1, N_QT, PAIR, TQS), lambda b, p, i: (b, p, i, 0, 0)),
        pl.BlockSpec((1, 1, nb, T, PAIR), lambda b, p, i: (b, p, 0, 0, 0)),
        pl.BlockSpec((1, 1, nb, vt_rows, T), lambda b, p, i: (b, p, 0, 0, 0)),
        pl.BlockSpec((1, 2, 2, T, T), lambda b, p, i: (p, 0, 0, 0, 0)),
    ]


def _diff_attn_kernel(qt_ref, k_ref, vt_ref, bias_ref, lam_ref, subln_ref, o_ref, *scratch):
    r = _make_refs(k_ref, vt_ref, scratch)
    _fill_q2(r.q2, qt_ref)
    _attend(r, pl.program_id(2), [slice(0, VT_ROWS_A)] * 2, bias_ref, None)

    lp = lam_ref[...]
    lam = (jnp.exp(jnp.sum(lp[0:1] * lp[1:2], axis=1, keepdims=True))
           - jnp.exp(jnp.sum(lp[2:3] * lp[3:4], axis=1, keepdims=True)) + LAM_INIT_A)
    o_all = r.acc[0:DV_A, 0:COLS] * (1.0 / r.acc[DV_A:DV_A + 1, 0:COLS])
    o = o_all[:, 0:TQ] - lam * o_all[:, TQ:COLS]
    o = o * lax.rsqrt(jnp.mean(o * o, axis=0, keepdims=True) + LN_EPS)
    o = o * subln_ref[...] * (1.0 - LAM_INIT_A)
    o_ref[0] = o.T.astype(BF16)


def _diff_attn(qt, k, vt, bias, lam_p, subln_col):
    bsz, _, nqs, _, _ = qt.shape
    nb = k.shape[2]
    return pl.pallas_call(
        _diff_attn_kernel,
        grid=(bsz, N_PAIRS, nqs // N_QT),
        in_specs=_attn_in_specs(nb, VT_ROWS_A) + [
            pl.BlockSpec((4, HEAD_DIM), lambda b, p, i: (0, 0)),
            pl.BlockSpec((PAIR, 1), lambda b, p, i: (0, 0)),
        ],
        out_specs=pl.BlockSpec((1, TQ, PAIR), lambda b, p, i: (b, i, p)),
        out_shape=jax.ShapeDtypeStruct((bsz, nqs * TQS, D_MODEL), BF16),
        scratch_shapes=_attn_scratch(VT_ROWS_A),
        compiler_params=_params(3),
        name="diff_attn",
    )(qt, k, vt, bias, lam_p, subln_col)


def _moba_attn_kernel(qt_ref, k_ref, vt_ref, bias_ref, mask_ref, o_ref, *scratch):
    r = _make_refs(k_ref, vt_ref, scratch)
    _fill_q2(r.q2, qt_ref)

    def mask_row(blk, c):
        g, s = divmod(c, N_SUB)
        t, s_in = divmod(s, TQS // T)
        col0 = g * TQS + s_in * T
        return mask_ref[0, 0, t, pl.ds(blk, 1), col0:col0 + T]

    rows = [slice(0, VT_ROWS_B), slice(VT_ROWS_B, 2 * VT_ROWS_B)]
    _attend(r, pl.program_id(2), rows, bias_ref, mask_row)

    o_all = r.acc[0:HEAD_DIM, 0:COLS] * (1.0 / r.acc[HEAD_DIM:HEAD_DIM + 1, 0:COLS])
    o = jnp.concatenate([o_all[:, 0:TQ], o_all[:, TQ:COLS]], axis=0)
    o_ref[0] = o.T.astype(BF16)


def _moba_attn(qt, k, vt, bias, mask):
    bsz, _, nqs, _, _ = qt.shape
    nb = k.shape[2]
    return pl.pallas_call(
        _moba_attn_kernel,
        grid=(bsz, N_PAIRS, nqs // N_QT),
        in_specs=_attn_in_specs(nb, 2 * VT_ROWS_B) + [
            pl.BlockSpec((1, 1, N_QT, nb, 2 * TQS), lambda b, p, i: (b, p, i, 0, 0)),
        ],
        out_specs=pl.BlockSpec((1, TQ, PAIR), lambda b, p, i: (b, i, p)),
        out_shape=jax.ShapeDtypeStruct((bsz, nqs * TQS, D_MODEL), BF16),
        scratch_shapes=_attn_scratch(VT_ROWS_B),
        compiler_params=_params(3),
        name="moba_attn",
    )(qt, k, vt, bias, mask)


def _post_attn_kernel(o_ref, h_ref, wo_ref, g1_ref, b1_ref, wup_ref, wdn_ref, g2_ref, b2_ref,
                      out_ref):
    y = ALPHA * h_ref[...] + jnp.dot(o_ref[...], wo_ref[...], preferred_element_type=F32)
    h = _layer_norm(y, g1_ref[...], b1_ref[...])
    hb = h.astype(BF16)
    y = ALPHA * h
    for c in range(D_FF // D_MODEL):
        cols = slice(c * D_MODEL, (c + 1) * D_MODEL)
        u = jnp.dot(hb, wup_ref[:, cols], preferred_element_type=F32)
        u = jnp.square(jnp.maximum(u, 0.0)).astype(BF16)
        y = y + jnp.dot(u, wdn_ref[cols, :], preferred_element_type=F32)
    out_ref[...] = _layer_norm(y, g2_ref[...], b2_ref[...])


def _post_attn(o, h, wo, g1, b1, wup, wdn, g2, b2):
    m = h.shape[0]
    tile = pl.BlockSpec((TM, D_MODEL), lambda i: (i, 0))
    return pl.pallas_call(
        _post_attn_kernel,
        grid=(m // TM,),
        in_specs=[
            tile,
            tile,
            _weight_spec((D_MODEL, D_MODEL)),
            _weight_spec((1, D_MODEL)),
            _weight_spec((1, D_MODEL)),
            _weight_spec((D_MODEL, D_FF)),
            _weight_spec((D_FF, D_MODEL)),
            _weight_spec((1, D_MODEL)),
            _weight_spec((1, D_MODEL)),
        ],
        out_specs=tile,
        out_shape=jax.ShapeDtypeStruct((m, D_MODEL), F32),
        compiler_params=_params(1),
        name="post_attn",
    )(o, h, wo, g1, b1, wup, wdn, g2, b2)


def kernel(x, w_in_a, lam_a, subln_a, w_out_a, w_kv_shared, w_q_b, w_out_b, rel_bias,
           ln1_g, ln1_b, ln2_g, ln2_b, w_up, w_down):
    bsz, seq, d = x.shape
    assert d == D_MODEL and seq % TM == 0 and TM % T == 0
    assert w_in_a.shape[0] == 1 and w_q_b.shape[0] == 1 and w_up.shape[0] == DEPTH
    m = bsz * seq
    row = lambda v: v.reshape(1, D_MODEL)

    bias = _bias_tiles(rel_bias)

    w_in = w_in_a[0]
    wqt = w_in[:, 0:D_MODEL].T.astype(BF16)
    wk = w_in[:, D_MODEL:2 * D_MODEL].astype(BF16)
    wvt = w_in[:, 2 * D_MODEL:].T.astype(BF16)
    qt, k, vt = _proj_a(x, wqt, wk, wvt)
    o = _diff_attn(qt, k, vt, bias, lam_a[0], subln_a[0].reshape(PAIR, 1))
    h = _post_attn(o.reshape(m, d), x.reshape(m, d), w_out_a[0].astype(BF16),
                   row(ln1_g[0]), row(ln1_b[0]), w_up[0].astype(BF16), w_down[0].astype(BF16),
                   row(ln2_g[0]), row(ln2_b[0]))

    h3 = h.reshape(bsz, seq, d)
    k2, vt2, kmean = _proj_kv(h3, w_kv_shared[:, 0:D_MODEL].astype(BF16),
                              w_kv_shared[:, D_MODEL:].T.astype(BF16))
    qt2, mask = _proj_q_gate(h3, w_q_b[0].T.astype(BF16), kmean)
    o = _moba_attn(qt2, k2, vt2, bias, mask)
    h = _post_attn(o.reshape(m, d), h, w_out_b[0].astype(BF16), row(ln1_g[1]), row(ln1_b[1]),
                   w_up[1].astype(BF16), w_down[1].astype(BF16), row(ln2_g[1]), row(ln2_b[1]))
    return h.reshape(bsz, seq, d)
```

```python
import functools
import math

import jax
import jax.numpy as jnp
from jax import lax
from jax.experimental import pallas as pl
from jax.experimental.pallas import tpu as pltpu

F32 = jnp.float32
BF16 = jnp.bfloat16

D_MODEL = 1024
D_FF = 4 * D_MODEL
DEPTH = 2
HEAD_DIM = 64
PAIR = 2 * HEAD_DIM
N_PAIRS = D_MODEL // PAIR
T = 256
TQS = 2 * T
N_SUB = 8
TQ = N_SUB * T
N_QT = TQ // TQS
COLS = 2 * TQ
N_CHUNKS = 2 * N_SUB
LANES = 128
COLS_PADDED = COLS + 0 * LANES
BF16_ROWS = 16
MOBA_TOPK = 3
REL_BUCKETS = 32
REL_MAX_DIST = 128
LN_EPS = 1e-5
ALPHA = (2 * DEPTH) ** 0.25
LAM_INIT_A = 0.8 - 0.6 * math.exp(-0.3 * 0)
LOG2E = 1.0 / math.log(2.0)
Q_SCALE = HEAD_DIM ** -0.5 * LOG2E
NEG = -1e30
V7X_VMEM_LIMIT = 56 * 1024 * 1024
TM = TQS
DV_A = PAIR
VT_ROWS_A = DV_A + BF16_ROWS
VT_ROWS_B = HEAD_DIM + BF16_ROWS


def _params(n_axes, flags=None):
    return pltpu.CompilerParams(
        dimension_semantics=("arbitrary",) * n_axes,
        vmem_limit_bytes=V7X_VMEM_LIMIT,
        flags=flags,
    )


def _nt_dot(a, b):
    return lax.dot_general(a, b, (((1,), (1,)), ((), ())), preferred_element_type=F32)


def _split_dot(a, b):
    a_hi = a.astype(BF16)
    a_lo = (a - a_hi.astype(F32)).astype(BF16)
    b_hi = b.astype(BF16)
    b_lo = (b - b_hi.astype(F32)).astype(BF16)
    dot = functools.partial(jnp.dot, preferred_element_type=F32)
    return dot(a_hi, b_hi) + (dot(a_hi, b_lo) + dot(a_lo, b_hi))


def _layer_norm(y, g, b):
    mu = jnp.mean(y, axis=-1, keepdims=True)
    yc = y - mu
    var = jnp.mean(yc * yc, axis=-1, keepdims=True)
    return yc * lax.rsqrt(var + LN_EPS) * g + b


def _chunk(c):
    return slice(c * T, (c + 1) * T)


def _bias_tile_kernel(tab_ref, out_ref):
    pr = pl.program_id(0)
    j = lax.broadcasted_iota(jnp.int32, (T, T), 0)
    i = lax.broadcasted_iota(jnp.int32, (T, T), 1)
    max_exact = REL_BUCKETS // 2
    for o in range(2):
        d = i - j + o * T
        n = jnp.maximum(d, 0)
        nf = jnp.maximum(n, max_exact).astype(F32)
        large = max_exact + (jnp.log(nf / max_exact) / math.log(REL_MAX_DIST / max_exact)
                             * (REL_BUCKETS - max_exact)).astype(jnp.int32)
        large = jnp.minimum(large, REL_BUCKETS - 1)
        bucket = jnp.where(n < max_exact, n, large)
        for m in range(2):
            col = 2 * pr + m
            far = tab_ref[REL_BUCKETS - 1, col]
            val = jnp.zeros((T, T), F32)
            for b in range(REL_BUCKETS - 1):
                val = jnp.where(bucket == b, (tab_ref[b, col] - far) * LOG2E, val)
            if o == 0:
                val = jnp.where(d >= 0, val, NEG)
            out_ref[0, o, m] = val


def _bias_tiles(rel_bias):
    return pl.pallas_call(
        _bias_tile_kernel,
        grid=(N_PAIRS,),
        in_specs=[pl.BlockSpec(memory_space=pltpu.SMEM)],
        out_specs=pl.BlockSpec((1, 2, 2, T, T), lambda p: (p, 0, 0, 0, 0)),
        out_shape=jax.ShapeDtypeStruct((N_PAIRS, 2, 2, T, T), F32),
        compiler_params=_params(1),
        name="bias_tiles",
    )(rel_bias)


def _store_q_tile(ref, qt):
    for p in range(N_PAIRS):
        ref[0, p, 0] = qt[p * PAIR:(p + 1) * PAIR].astype(BF16)


def _store_k_blocks(ref, k):
    for t in range(TM // T):
        for p in range(N_PAIRS):
            ref[0, p, t] = k[t * T:(t + 1) * T, p * PAIR:(p + 1) * PAIR].astype(BF16)


def _store_vt_blocks(ref, vt, width):
    groups = PAIR // width
    ones = jnp.ones((BF16_ROWS, T), BF16)
    for t in range(TM // T):
        for p in range(N_PAIRS):
            for g in range(groups):
                r0 = g * (width + BF16_ROWS)
                rows = slice(p * PAIR + g * width, p * PAIR + (g + 1) * width)
                ref[0, p, t, r0:r0 + width] = vt[rows, t * T:(t + 1) * T].astype(BF16)
                ref[0, p, t, r0 + width:r0 + width + BF16_ROWS] = ones


def _proj_a_kernel(x_ref, wqt_ref, wk_ref, wvt_ref, qt_ref, k_ref, vt_ref):
    xb = x_ref[0].astype(BF16)
    _store_q_tile(qt_ref, _nt_dot(wqt_ref[...], xb) * Q_SCALE)
    _store_k_blocks(k_ref, jnp.dot(xb, wk_ref[...], preferred_element_type=F32))
    _store_vt_blocks(vt_ref, _nt_dot(wvt_ref[...], xb), DV_A)


def _q_tile_spec():
    return pl.BlockSpec((1, N_PAIRS, 1, PAIR, TQS), lambda b, i: (b, 0, i, 0, 0))


def _k_blocks_spec():
    return pl.BlockSpec((1, N_PAIRS, TM // T, T, PAIR), lambda b, i: (b, 0, i, 0, 0))


def _vt_blocks_spec(rows):
    return pl.BlockSpec((1, N_PAIRS, TM // T, rows, T), lambda b, i: (b, 0, i, 0, 0))


def _weight_spec(shape):
    return pl.BlockSpec(shape, lambda *_: (0,) * len(shape), pipeline_mode=pl.Buffered(1))


def _proj_a(x, wqt, wk, wvt):
    bsz, seq, _ = x.shape
    nb = seq // T
    return pl.pallas_call(
        _proj_a_kernel,
        grid=(bsz, seq // TM),
        in_specs=[
            pl.BlockSpec((1, TM, D_MODEL), lambda b, i: (b, i, 0)),
            _weight_spec((D_MODEL, D_MODEL)),
            _weight_spec((D_MODEL, D_MODEL)),
            _weight_spec((D_MODEL, D_MODEL)),
        ],
        out_specs=[_q_tile_spec(), _k_blocks_spec(), _vt_blocks_spec(VT_ROWS_A)],
        out_shape=[
            jax.ShapeDtypeStruct((bsz, N_PAIRS, seq // TQS, PAIR, TQS), BF16),
            jax.ShapeDtypeStruct((bsz, N_PAIRS, nb, T, PAIR), BF16),
            jax.ShapeDtypeStruct((bsz, N_PAIRS, nb, VT_ROWS_A, T), BF16),
        ],
        compiler_params=_params(2),
        name="proj_a",
    )(x, wqt, wk, wvt)


def _proj_kv_kernel(h_ref, wk_ref, wvt_ref, k_ref, vt_ref, kmean_ref):
    hb = h_ref[0].astype(BF16)
    k = jnp.dot(hb, wk_ref[...], preferred_element_type=F32)
    _store_k_blocks(k_ref, k)
    for t in range(TM // T):
        blk = pl.program_id(1) * (TM // T) + t
        kmean_ref[0, pl.ds(blk, 1), :] = jnp.mean(k[t * T:(t + 1) * T], axis=0, keepdims=True)
    _store_vt_blocks(vt_ref, _nt_dot(wvt_ref[...], hb), HEAD_DIM)


def _proj_kv(h, wk, wvt):
    bsz, seq, _ = h.shape
    nb = seq // T
    return pl.pallas_call(
        _proj_kv_kernel,
        grid=(bsz, seq // TM),
        in_specs=[
            pl.BlockSpec((1, TM, D_MODEL), lambda b, i: (b, i, 0)),
            _weight_spec((D_MODEL, D_MODEL)),
            _weight_spec((D_MODEL, D_MODEL)),
        ],
        out_specs=[
            _k_blocks_spec(),
            _vt_blocks_spec(2 * VT_ROWS_B),
            pl.BlockSpec((1, nb, D_MODEL), lambda b, i: (b, 0, 0)),
        ],
        out_shape=[
            jax.ShapeDtypeStruct((bsz, N_PAIRS, nb, T, PAIR), BF16),
            jax.ShapeDtypeStruct((bsz, N_PAIRS, nb, 2 * VT_ROWS_B, T), BF16),
            jax.ShapeDtypeStruct((bsz, nb, D_MODEL), F32),
        ],
        compiler_params=_params(2),
        name="proj_kv",
    )(h, wk, wvt)


def _proj_q_gate_kernel(h_ref, wqt_ref, kmean_ref, qt_ref, mask_ref, *, n_blocks):
    hb = h_ref[0].astype(BF16)
    qt = _nt_dot(wqt_ref[...], hb)
    _store_q_tile(qt_ref, qt * Q_SCALE)

    km = kmean_ref[0]
    lane = lax.broadcasted_iota(jnp.int32, (n_blocks, PAIR), 1)
    blk = lax.broadcasted_iota(jnp.int32, (n_blocks, TM), 0)
    col = lax.broadcasted_iota(jnp.int32, (1, TM), 1)
    q_blk = pl.program_id(1) * (TM // T) + lax.shift_right_logical(col, T.bit_length() - 1)
    past = blk < q_blk
    blk_f = blk.astype(F32)
    for p in range(N_PAIRS):
        km_p = km[:, p * PAIR:(p + 1) * PAIR]
        km2 = jnp.concatenate(
            [jnp.where(lane < HEAD_DIM, km_p, 0.0), jnp.where(lane >= HEAD_DIM, km_p, 0.0)],
            axis=0)
        g2 = _split_dot(km2, qt[p * PAIR:(p + 1) * PAIR])
        for sub in range(2):
            g = jnp.where(past, g2[sub * n_blocks:(sub + 1) * n_blocks], -jnp.inf)
            keep = jnp.zeros((n_blocks, TM), F32)
            for _ in range(MOBA_TOPK):
                top = jnp.max(g, axis=0, keepdims=True)
                first = jnp.min(jnp.where(g == top, blk_f, float(n_blocks)), axis=0, keepdims=True)
                pick = (blk_f == first) & (top > -jnp.inf)
                keep = jnp.where(pick, 1.0, keep)
                g = jnp.where(pick, -jnp.inf, g)
            mask_ref[0, p, 0, :, sub * TQS:(sub + 1) * TQS] = jnp.where(keep > 0.0, 0.0, NEG)


def _proj_q_gate(h, wqt, kmean):
    bsz, seq, _ = h.shape
    nb = seq // T
    return pl.pallas_call(
        functools.partial(_proj_q_gate_kernel, n_blocks=nb),
        grid=(bsz, seq // TM),
        in_specs=[
            pl.BlockSpec((1, TM, D_MODEL), lambda b, i: (b, i, 0)),
            _weight_spec((D_MODEL, D_MODEL)),
            pl.BlockSpec((1, nb, D_MODEL), lambda b, i: (b, 0, 0)),
        ],
        out_specs=[
            _q_tile_spec(),
            pl.BlockSpec((1, N_PAIRS, 1, nb, 2 * TQS), lambda b, i: (b, 0, i, 0, 0)),
        ],
        out_shape=[
            jax.ShapeDtypeStruct((bsz, N_PAIRS, seq // TQS, PAIR, TQS), BF16),
            jax.ShapeDtypeStruct((bsz, N_PAIRS, seq // TQS, nb, 2 * TQS), F32),
        ],
        compiler_params=_params(2),
        name="proj_q_gate",
    )(h, wqt, kmean)


class _AttnRefs:
    def __init__(self, k_ref, vt_ref, q2, s, mb, p, al, m, acc):
        self.k, self.vt, self.q2 = k_ref, vt_ref, q2
        self.s, self.mb, self.p, self.al, self.m, self.acc = s, mb, p, al, m, acc


def _fill_q2(q2_ref, qt_ref):
    zeros = jnp.zeros((HEAD_DIM, TQ), BF16)
    q2_ref[0:HEAD_DIM, TQ:COLS] = zeros
    q2_ref[HEAD_DIM:PAIR, 0:TQ] = zeros
    for t in range(N_QT):
        qt = qt_ref[0, 0, t]
        q2_ref[0:HEAD_DIM, t * TQS:(t + 1) * TQS] = qt[0:HEAD_DIM]
        q2_ref[HEAD_DIM:PAIR, TQ + t * TQS:TQ + (t + 1) * TQS] = qt[HEAD_DIM:PAIR]


def _stage_a(r, par, blk, extras):
    kj = r.k[0, 0, blk]
    for c in range(N_CHUNKS):
        if isinstance(extras[c], str):
            continue
        s = jnp.dot(kj, r.q2[:, _chunk(c)], preferred_element_type=F32)
        if extras[c] is not None:
            s = s + extras[c]
        r.s[par][:, _chunk(c)] = s
        r.mb[par][:, _chunk(c)] = jnp.max(s, axis=0, keepdims=True)


def _stage_b(r, par, chunks):
    for c in chunks:
        m_old = r.m[:, _chunk(c)]
        m_new = jnp.maximum(m_old, r.mb[par][:, _chunk(c)])
        r.al[par][:, _chunk(c)] = jnp.exp2(m_old - m_new)
        r.m[:, _chunk(c)] = m_new
        r.p[par][:, _chunk(c)] = jnp.exp2(r.s[par][:, _chunk(c)] - m_new).astype(BF16)


def _stage_c(r, par, blk, vt_rows, chunks):
    vt = r.vt[0, 0, blk]
    for c in chunks:
        pv = jnp.dot(vt[vt_rows[c // N_SUB]], r.p[par][:, _chunk(c)], preferred_element_type=F32)
        r.acc[:, _chunk(c)] = r.al[par][:, _chunk(c)] * r.acc[:, _chunk(c)] + pv


ALL_CHUNKS = tuple(range(N_CHUNKS))


def _live_chunks(visit):
    if visit >= N_SUB:
        return ALL_CHUNKS
    return tuple(c for c in ALL_CHUNKS if c % N_SUB >= visit)


def _visit_extras(bias_ref, offset_of, mask_of):
    extras = []
    for c in range(N_CHUNKS):
        g, s = divmod(c, N_SUB)
        off = offset_of(s)
        if off < 0:
            extras.append("future")
            continue
        term = bias_ref[0, off, g] if off <= 1 else None
        mask = mask_of(c) if off >= 1 else None
        if term is None:
            extras.append(mask)
        else:
            extras.append(term if mask is None else term + mask)
    return extras


def _attend(r, tile, vt_rows, bias_ref, mask_row):
    r.m[...] = jnp.full(r.m.shape, NEG, F32)
    r.acc[...] = jnp.zeros(r.acc.shape, F32)
    base = N_SUB * tile
    far = 2

    def mask_of(blk):
        return (lambda c: None) if mask_row is None else (lambda c: mask_row(blk, c))

    def block_of(n):
        return jnp.where(n < N_SUB, base + n, jnp.where(n == N_SUB, base - 1, n - N_SUB - 1))

    def stage(n, blk, extras):
        _stage_a(r, n % 2, blk, extras)
        if n >= 1:
            _stage_b(r, (n - 1) % 2, _live_chunks(n - 1))
        if n >= 2:
            _stage_c(r, n % 2, base + n - 2, vt_rows, _live_chunks(n - 2))

    for d in range(N_SUB):
        stage(d, base + d, _visit_extras(bias_ref, lambda s: s - d, mask_of(base + d)))

    @pl.when(tile == 0)
    def _():
        _stage_b(r, 1, _live_chunks(N_SUB - 1))
        _stage_c(r, 0, base + N_SUB - 2, vt_rows, _live_chunks(N_SUB - 2))
        _stage_c(r, 1, base + N_SUB - 1, vt_rows, _live_chunks(N_SUB - 1))

    @pl.when(tile > 0)
    def _():
        prev = _visit_extras(bias_ref, lambda s: 1 if s == 0 else far, mask_of(base - 1))
        stage(N_SUB, base - 1, prev)
        stage(N_SUB + 1, 0, _visit_extras(bias_ref, lambda s: far, mask_of(0)))

        def body(t, carry):
            n = N_SUB + 2 + 2 * t
            first = 1 + 2 * t
            _stage_a(r, 0, first, _visit_extras(bias_ref, lambda s: far, mask_of(first)))
            _stage_b(r, 1, ALL_CHUNKS)
            _stage_c(r, 0, block_of(n - 2), vt_rows, ALL_CHUNKS)
            _stage_a(r, 1, first + 1, _visit_extras(bias_ref, lambda s: far, mask_of(first + 1)))
            _stage_b(r, 0, ALL_CHUNKS)
            _stage_c(r, 1, first - 1, vt_rows, ALL_CHUNKS)
            return carry

        lax.fori_loop(0, base // 2 - 1, body, 0)

        n_visits = base + N_SUB
        _stage_b(r, 1, ALL_CHUNKS)
        _stage_c(r, 0, block_of(n_visits - 2), vt_rows, ALL_CHUNKS)
        _stage_c(r, 1, block_of(n_visits - 1), vt_rows, ALL_CHUNKS)


def _attn_scratch(acc_rows):
    return [
        pltpu.VMEM((PAIR, COLS_PADDED), BF16),
        pltpu.VMEM((T, COLS_PADDED + 2 * LANES), F32),
        pltpu.VMEM((T, COLS_PADDED + 2 * LANES), F32),
        pltpu.VMEM((1, COLS), F32),
        pltpu.VMEM((1, COLS), F32),
        pltpu.VMEM((T, COLS_PADDED), BF16),
        pltpu.VMEM((T, COLS_PADDED), BF16),
        pltpu.VMEM((1, COLS), F32),
        pltpu.VMEM((1, COLS), F32),
        pltpu.VMEM((1, COLS), F32),
        pltpu.VMEM((acc_rows, COLS_PADDED + 2 * LANES), F32),
    ]


def _make_refs(k_ref, vt_ref, scratch):
    q2, s0, s1, mb0, mb1, p0, p1, al0, al1, m, acc = scratch
    return _AttnRefs(k_ref, vt_ref, q2, (s0, s1), (mb0, mb1), (p0, p1), (al0, al1), m, acc)


def _attn_in_specs(nb, vt_rows):
    return [
        pl.BlockSpec((1, 1, N_QT, PAIR, TQS), lambda b, p, i: (b, p, i, 0, 0)),
        pl.BlockSpec((1, 1, nb, T, PAIR), lambda b, p, i: (b, p, 0, 0, 0)),
        pl.BlockSpec((1, 1, nb, vt_rows, T), lambda b, p, i: (b, p, 0, 0, 0)),
        pl.BlockSpec((1, 2, 2, T, T), lambda b, p, i: (p, 0, 0, 0, 0)),
    ]


def _diff_attn_kernel(qt_ref, k_ref, vt_ref, bias_ref, lam_ref, subln_ref, o_ref, *scratch):
    r = _make_refs(k_ref, vt_ref, scratch)
    _fill_q2(r.q2, qt_ref)
    _attend(r, pl.program_id(2), [slice(0, VT_ROWS_A)] * 2, bias_ref, None)

    lp = lam_ref[...]
    lam = (jnp.exp(jnp.sum(lp[0:1] * lp[1:2], axis=1, keepdims=True))
           - jnp.exp(jnp.sum(lp[2:3] * lp[3:4], axis=1, keepdims=True)) + LAM_INIT_A)
    o_all = r.acc[0:DV_A, 0:COLS] * (1.0 / r.acc[DV_A:DV_A + 1, 0:COLS])
    o = o_all[:, 0:TQ] - lam * o_all[:, TQ:COLS]
    o = o * lax.rsqrt(jnp.mean(o * o, axis=0, keepdims=True) + LN_EPS)
    o = o * subln_ref[...] * (1.0 - LAM_INIT_A)
    o_ref[0] = o.T.astype(BF16)


def _diff_attn(qt, k, vt, bias, lam_p, subln_col):
    bsz, _, nqs, _, _ = qt.shape
    nb = k.shape[2]
    return pl.pallas_call(
        _diff_attn_kernel,
        grid=(bsz, N_PAIRS, nqs // N_QT),
        in_specs=_attn_in_specs(nb, VT_ROWS_A) + [
            pl.BlockSpec((4, HEAD_DIM), lambda b, p, i: (0, 0)),
            pl.BlockSpec((PAIR, 1), lambda b, p, i: (0, 0)),
        ],
        out_specs=pl.BlockSpec((1, TQ, PAIR), lambda b, p, i: (b, i, p)),
        out_shape=jax.ShapeDtypeStruct((bsz, nqs * TQS, D_MODEL), BF16),
        scratch_shapes=_attn_scratch(VT_ROWS_A),
        compiler_params=_params(3),
        name="diff_attn",
    )(qt, k, vt, bias, lam_p, subln_col)


def _moba_attn_kernel(qt_ref, k_ref, vt_ref, bias_ref, mask_ref, o_ref, *scratch):
    r = _make_refs(k_ref, vt_ref, scratch)
    _fill_q2(r.q2, qt_ref)

    def mask_row(blk, c):
        g, s = divmod(c, N_SUB)
        t, s_in = divmod(s, TQS // T)
        col0 = g * TQS + s_in * T
        return mask_ref[0, 0, t, pl.ds(blk, 1), col0:col0 + T]

    rows = [slice(0, VT_ROWS_B), slice(VT_ROWS_B, 2 * VT_ROWS_B)]
    _attend(r, pl.program_id(2), rows, bias_ref, mask_row)

    o_all = r.acc[0:HEAD_DIM, 0:COLS] * (1.0 / r.acc[HEAD_DIM:HEAD_DIM + 1, 0:COLS])
    o = jnp.concatenate([o_all[:, 0:TQ], o_all[:, TQ:COLS]], axis=0)
    o_ref[0] = o.T.astype(BF16)


def _moba_attn(qt, k, vt, bias, mask):
    bsz, _, nqs, _, _ = qt.shape
    nb = k.shape[2]
    return pl.pallas_call(
        _moba_attn_kernel,
        grid=(bsz, N_PAIRS, nqs // N_QT),
        in_specs=_attn_in_specs(nb, 2 * VT_ROWS_B) + [
            pl.BlockSpec((1, 1, N_QT, nb, 2 * TQS), lambda b, p, i: (b, p, i, 0, 0)),
        ],
        out_specs=pl.BlockSpec((1, TQ, PAIR), lambda b, p, i: (b, i, p)),
        out_shape=jax.ShapeDtypeStruct((bsz, nqs * TQS, D_MODEL), BF16),
        scratch_shapes=_attn_scratch(VT_ROWS_B),
        compiler_params=_params(3),
        name="moba_attn",
    )(qt, k, vt, bias, mask)


def _post_attn_kernel(o_ref, h_ref, wo_ref, g1_ref, b1_ref, wup_ref, wdn_ref, g2_ref, b2_ref,
                      out_ref):
    y = ALPHA * h_ref[...] + jnp.dot(o_ref[...], wo_ref[...], preferred_element_type=F32)
    h = _layer_norm(y, g1_ref[...], b1_ref[...])
    hb = h.astype(BF16)
    y = ALPHA * h
    for c in range(D_FF // D_MODEL):
        cols = slice(c * D_MODEL, (c + 1) * D_MODEL)
        u = jnp.dot(hb, wup_ref[:, cols], preferred_element_type=F32)
        u = jnp.square(jnp.maximum(u, 0.0)).astype(BF16)
        y = y + jnp.dot(u, wdn_ref[cols, :], preferred_element_type=F32)
    out_ref[...] = _layer_norm(y, g2_ref[...], b2_ref[...])


def _post_attn(o, h, wo, g1, b1, wup, wdn, g2, b2):
    m = h.shape[0]
    tile = pl.BlockSpec((TM, D_MODEL), lambda i: (i, 0))
    return pl.pallas_call(
        _post_attn_kernel,
        grid=(m // TM,),
        in_specs=[
            tile,
            tile,
            _weight_spec((D_MODEL, D_MODEL)),
            _weight_spec((1, D_MODEL)),
            _weight_spec((1, D_MODEL)),
            _weight_spec((D_MODEL, D_FF)),
            _weight_spec((D_FF, D_MODEL)),
            _weight_spec((1, D_MODEL)),
            _weight_spec((1, D_MODEL)),
        ],
        out_specs=tile,
        out_shape=jax.ShapeDtypeStruct((m, D_MODEL), F32),
        compiler_params=_params(1),
        name="post_attn",
    )(o, h, wo, g1, b1, wup, wdn, g2, b2)


def kernel(x, w_in_a, lam_a, subln_a, w_out_a, w_kv_shared, w_q_b, w_out_b, rel_bias,
           ln1_g, ln1_b, ln2_g, ln2_b, w_up, w_down):
    bsz, seq, d = x.shape
    assert d == D_MODEL and seq % TM == 0 and TM % T == 0
    assert w_in_a.shape[0] == 1 and w_q_b.shape[0] == 1 and w_up.shape[0] == DEPTH
    m = bsz * seq
    row = lambda v: v.reshape(1, D_MODEL)

    bias = _bias_tiles(rel_bias)

    w_in = w_in_a[0]
    wqt = w_in[:, 0:D_MODEL].T.astype(BF16)
    wk = w_in[:, D_MODEL:2 * D_MODEL].astype(BF16)
    wvt = w_in[:, 2 * D_MODEL:].T.astype(BF16)
    qt, k, vt = _proj_a(x, wqt, wk, wvt)
    o = _diff_attn(qt, k, vt, bias, lam_a[0], subln_a[0].reshape(PAIR, 1))
    h = _post_attn(o.reshape(m, d), x.reshape(m, d), w_out_a[0].astype(BF16),
                   row(ln1_g[0]), row(ln1_b[0]), w_up[0].astype(BF16), w_down[0].astype(BF16),
                   row(ln2_g[0]), row(ln2_b[0]))

    h3 = h.reshape(bsz, seq, d)
    k2, vt2, kmean = _proj_kv(h3, w_kv_shared[:, 0:D_MODEL].astype(BF16),
                              w_kv_shared[:, D_MODEL:].T.astype(BF16))
    qt2, mask = _proj_q_gate(h3, w_q_b[0].T.astype(BF16), kmean)
    o = _moba_attn(qt2, k2, vt2, bias, mask)
    h = _post_attn(o.reshape(m, d), h, w_out_b[0].astype(BF16), row(ln1_g[1]), row(ln1_b[1]),
                   w_up[1].astype(BF16), w_down[1].astype(BF16), row(ln2_g[1]), row(ln2_b[1]))
    return h.reshape(bsz, seq, d)
```

```python
import functools
import math

import jax
import jax.numpy as jnp
from jax import lax
from jax.experimental import pallas as pl
from jax.experimental.pallas import tpu as pltpu

F32 = jnp.float32
BF16 = jnp.bfloat16

D_MODEL = 1024
D_FF = 4 * D_MODEL
DEPTH = 2
HEAD_DIM = 64
PAIR = 2 * HEAD_DIM
N_PAIRS = D_MODEL // PAIR
T = 256
TQS = 2 * T
N_SUB = 8
TQ = N_SUB * T
N_QT = TQ // TQS
COLS = 2 * TQ
N_CHUNKS = 2 * N_SUB
LANES = 128
COLS_F32_PITCH = COLS + 2 * LANES
BF16_ROWS = 16
MOBA_TOPK = 3
REL_BUCKETS = 32
REL_MAX_DIST = 128
LN_EPS = 1e-5
ALPHA = (2 * DEPTH) ** 0.25
LAM_INIT_A = 0.8 - 0.6 * math.exp(-0.3 * 0)
LOG2E = 1.0 / math.log(2.0)
Q_SCALE = HEAD_DIM ** -0.5 * LOG2E
NEG = -1e30
V7X_VMEM_LIMIT = 56 * 1024 * 1024
TM = TQS
DV_A = PAIR
VT_ROWS_A = DV_A + BF16_ROWS
VT_ROWS_B = HEAD_DIM + BF16_ROWS


def _params(n_axes):
    return pltpu.CompilerParams(
        dimension_semantics=("arbitrary",) * n_axes,
        vmem_limit_bytes=V7X_VMEM_LIMIT,
    )


def _nt_dot(a, b):
    return lax.dot_general(a, b, (((1,), (1,)), ((), ())), preferred_element_type=F32)


def _split_dot(a, b):
    a_hi = a.astype(BF16)
    a_lo = (a - a_hi.astype(F32)).astype(BF16)
    b_hi = b.astype(BF16)
    b_lo = (b - b_hi.astype(F32)).astype(BF16)
    dot = functools.partial(jnp.dot, preferred_element_type=F32)
    return dot(a_hi, b_hi) + (dot(a_hi, b_lo) + dot(a_lo, b_hi))


def _layer_norm(y, g, b):
    mu = jnp.mean(y, axis=-1, keepdims=True)
    yc = y - mu
    var = jnp.mean(yc * yc, axis=-1, keepdims=True)
    return yc * lax.rsqrt(var + LN_EPS) * g + b


def _chunk(c):
    return slice(c * T, (c + 1) * T)


def _bias_tile_kernel(tab_ref, out_ref):
    pr = pl.program_id(0)
    j = lax.broadcasted_iota(jnp.int32, (T, T), 0)
    i = lax.broadcasted_iota(jnp.int32, (T, T), 1)
    max_exact = REL_BUCKETS // 2
    for o in range(2):
        d = i - j + o * T
        n = jnp.maximum(d, 0)
        nf = jnp.maximum(n, max_exact).astype(F32)
        large = max_exact + (jnp.log(nf / max_exact) / math.log(REL_MAX_DIST / max_exact)
                             * (REL_BUCKETS - max_exact)).astype(jnp.int32)
        large = jnp.minimum(large, REL_BUCKETS - 1)
        bucket = jnp.where(n < max_exact, n, large)
        for m in range(2):
            col = 2 * pr + m
            far = tab_ref[REL_BUCKETS - 1, col]
            val = jnp.zeros((T, T), F32)
            for b in range(REL_BUCKETS - 1):
                val = jnp.where(bucket == b, (tab_ref[b, col] - far) * LOG2E, val)
            if o == 0:
                val = jnp.where(d >= 0, val, NEG)
            out_ref[0, o, m] = val


def _bias_tiles(rel_bias):
    return pl.pallas_call(
        _bias_tile_kernel,
        grid=(N_PAIRS,),
        in_specs=[pl.BlockSpec(memory_space=pltpu.SMEM)],
        out_specs=pl.BlockSpec((1, 2, 2, T, T), lambda p: (p, 0, 0, 0, 0)),
        out_shape=jax.ShapeDtypeStruct((N_PAIRS, 2, 2, T, T), F32),
        compiler_params=_params(1),
        name="bias_tiles",
    )(rel_bias)


def _store_q_tile(ref, qt):
    for p in range(N_PAIRS):
        ref[0, p, 0] = qt[p * PAIR:(p + 1) * PAIR].astype(BF16)


def _store_k_blocks(ref, k):
    for t in range(TM // T):
        for p in range(N_PAIRS):
            ref[0, p, t] = k[t * T:(t + 1) * T, p * PAIR:(p + 1) * PAIR].astype(BF16)


def _store_vt_blocks(ref, vt, width):
    groups = PAIR // width
    ones = jnp.ones((BF16_ROWS, T), BF16)
    for t in range(TM // T):
        for p in range(N_PAIRS):
            for g in range(groups):
                r0 = g * (width + BF16_ROWS)
                rows = slice(p * PAIR + g * width, p * PAIR + (g + 1) * width)
                ref[0, p, t, r0:r0 + width] = vt[rows, t * T:(t + 1) * T].astype(BF16)
                ref[0, p, t, r0 + width:r0 + width + BF16_ROWS] = ones


def _proj_a_kernel(x_ref, wqt_ref, wk_ref, wvt_ref, qt_ref, k_ref, vt_ref):
    xb = x_ref[0].astype(BF16)
    _store_q_tile(qt_ref, _nt_dot(wqt_ref[...], xb) * Q_SCALE)
    _store_k_blocks(k_ref, jnp.dot(xb, wk_ref[...], preferred_element_type=F32))
    _store_vt_blocks(vt_ref, _nt_dot(wvt_ref[...], xb), DV_A)


def _q_tile_spec():
    return pl.BlockSpec((1, N_PAIRS, 1, PAIR, TQS), lambda b, i: (b, 0, i, 0, 0))


def _k_blocks_spec():
    return pl.BlockSpec((1, N_PAIRS, TM // T, T, PAIR), lambda b, i: (b, 0, i, 0, 0))


def _vt_blocks_spec(rows):
    return pl.BlockSpec((1, N_PAIRS, TM // T, rows, T), lambda b, i: (b, 0, i, 0, 0))


def _weight_spec(shape):
    return pl.BlockSpec(shape, lambda *_: (0,) * len(shape), pipeline_mode=pl.Buffered(1))


def _proj_a(x, wqt, wk, wvt):
    bsz, seq, _ = x.shape
    nb = seq // T
    return pl.pallas_call(
        _proj_a_kernel,
        grid=(bsz, seq // TM),
        in_specs=[
            pl.BlockSpec((1, TM, D_MODEL), lambda b, i: (b, i, 0)),
            _weight_spec((D_MODEL, D_MODEL)),
            _weight_spec((D_MODEL, D_MODEL)),
            _weight_spec((D_MODEL, D_MODEL)),
        ],
        out_specs=[_q_tile_spec(), _k_blocks_spec(), _vt_blocks_spec(VT_ROWS_A)],
        out_shape=[
            jax.ShapeDtypeStruct((bsz, N_PAIRS, seq // TQS, PAIR, TQS), BF16),
            jax.ShapeDtypeStruct((bsz, N_PAIRS, nb, T, PAIR), BF16),
            jax.ShapeDtypeStruct((bsz, N_PAIRS, nb, VT_ROWS_A, T), BF16),
        ],
        compiler_params=_params(2),
        name="proj_a",
    )(x, wqt, wk, wvt)


def _proj_kv_kernel(h_ref, wk_ref, wvt_ref, k_ref, vt_ref, kmean_ref):
    hb = h_ref[0].astype(BF16)
    k = jnp.dot(hb, wk_ref[...], preferred_element_type=F32)
    _store_k_blocks(k_ref, k)
    for t in range(TM // T):
        blk = pl.program_id(1) * (TM // T) + t
        kmean_ref[0, pl.ds(blk, 1), :] = jnp.mean(k[t * T:(t + 1) * T], axis=0, keepdims=True)
    _store_vt_blocks(vt_ref, _nt_dot(wvt_ref[...], hb), HEAD_DIM)


def _proj_kv(h, wk, wvt):
    bsz, seq, _ = h.shape
    nb = seq // T
    return pl.pallas_call(
        _proj_kv_kernel,
        grid=(bsz, seq // TM),
        in_specs=[
            pl.BlockSpec((1, TM, D_MODEL), lambda b, i: (b, i, 0)),
            _weight_spec((D_MODEL, D_MODEL)),
            _weight_spec((D_MODEL, D_MODEL)),
        ],
        out_specs=[
            _k_blocks_spec(),
            _vt_blocks_spec(2 * VT_ROWS_B),
            pl.BlockSpec((1, nb, D_MODEL), lambda b, i: (b, 0, 0)),
        ],
        out_shape=[
            jax.ShapeDtypeStruct((bsz, N_PAIRS, nb, T, PAIR), BF16),
            jax.ShapeDtypeStruct((bsz, N_PAIRS, nb, 2 * VT_ROWS_B, T), BF16),
            jax.ShapeDtypeStruct((bsz, nb, D_MODEL), F32),
        ],
        compiler_params=_params(2),
        name="proj_kv",
    )(h, wk, wvt)


def _proj_q_gate_kernel(h_ref, wqt_ref, kmean_ref, qt_ref, mask_ref, *, n_blocks):
    hb = h_ref[0].astype(BF16)
    qt = _nt_dot(wqt_ref[...], hb)
    _store_q_tile(qt_ref, qt * Q_SCALE)

    km = kmean_ref[0]
    lane = lax.broadcasted_iota(jnp.int32, (n_blocks, PAIR), 1)
    blk = lax.broadcasted_iota(jnp.int32, (n_blocks, TM), 0)
    col = lax.broadcasted_iota(jnp.int32, (1, TM), 1)
    q_blk = pl.program_id(1) * (TM // T) + lax.shift_right_logical(col, T.bit_length() - 1)
    past = blk < q_blk
    blk_f = blk.astype(F32)
    for p in range(N_PAIRS):
        km_p = km[:, p * PAIR:(p + 1) * PAIR]
        km2 = jnp.concatenate(
            [jnp.where(lane < HEAD_DIM, km_p, 0.0), jnp.where(lane >= HEAD_DIM, km_p, 0.0)],
            axis=0)
        g2 = _split_dot(km2, qt[p * PAIR:(p + 1) * PAIR])
        for sub in range(2):
            g = jnp.where(past, g2[sub * n_blocks:(sub + 1) * n_blocks], -jnp.inf)
            keep = jnp.zeros((n_blocks, TM), F32)
            for _ in range(MOBA_TOPK):
                top = jnp.max(g, axis=0, keepdims=True)
                first = jnp.min(jnp.where(g == top, blk_f, float(n_blocks)), axis=0, keepdims=True)
                pick = (blk_f == first) & (top > -jnp.inf)
                keep = jnp.where(pick, 1.0, keep)
                g = jnp.where(pick, -jnp.inf, g)
            mask_ref[0, p, 0, :, sub * TQS:(sub + 1) * TQS] = jnp.where(keep > 0.0, 0.0, NEG)


def _proj_q_gate(h, wqt, kmean):
    bsz, seq, _ = h.shape
    nb = seq // T
    return pl.pallas_call(
        functools.partial(_proj_q_gate_kernel, n_blocks=nb),
        grid=(bsz, seq // TM),
        in_specs=[
            pl.BlockSpec((1, TM, D_MODEL), lambda b, i: (b, i, 0)),
            _weight_spec((D_MODEL, D_MODEL)),
            pl.BlockSpec((1, nb, D_MODEL), lambda b, i: (b, 0, 0)),
        ],
        out_specs=[
            _q_tile_spec(),
            pl.BlockSpec((1, N_PAIRS, 1, nb, 2 * TQS), lambda b, i: (b, 0, i, 0, 0)),
        ],
        out_shape=[
            jax.ShapeDtypeStruct((bsz, N_PAIRS, seq // TQS, PAIR, TQS), BF16),
            jax.ShapeDtypeStruct((bsz, N_PAIRS, seq // TQS, nb, 2 * TQS), F32),
        ],
        compiler_params=_params(2),
        name="proj_q_gate",
    )(h, wqt, kmean)


class _AttnRefs:
    def __init__(self, k_ref, vt_ref, q2, s, mb, p, al, m, acc):
        self.k, self.vt, self.q2 = k_ref, vt_ref, q2
        self.s, self.mb, self.p, self.al, self.m, self.acc = s, mb, p, al, m, acc


def _fill_q2(q2_ref, qt_ref):
    zeros = jnp.zeros((HEAD_DIM, TQ), BF16)
    q2_ref[0:HEAD_DIM, TQ:COLS] = zeros
    q2_ref[HEAD_DIM:PAIR, 0:TQ] = zeros
    for t in range(N_QT):
        qt = qt_ref[0, 0, t]
        q2_ref[0:HEAD_DIM, t * TQS:(t + 1) * TQS] = qt[0:HEAD_DIM]
        q2_ref[HEAD_DIM:PAIR, TQ + t * TQS:TQ + (t + 1) * TQS] = qt[HEAD_DIM:PAIR]


def _stage_a(r, par, blk, extras):
    kj = r.k[0, 0, blk]
    for c in range(N_CHUNKS):
        if isinstance(extras[c], str):
            continue
        bias, mask = extras[c]
        s = jnp.dot(kj, r.q2[:, _chunk(c)], preferred_element_type=F32)
        if bias is not None:
            s = s + bias
        r.s[par][:, _chunk(c)] = s
        col_max = jnp.max(s, axis=0, keepdims=True)
        r.mb[par][:, _chunk(c)] = col_max if mask is None else col_max + mask()


def _stage_b(r, par, extras):
    for c in _live_chunks(extras):
        mask = extras[c][1]
        m_old = r.m[:, _chunk(c)]
        m_new = jnp.maximum(m_old, r.mb[par][:, _chunk(c)])
        r.al[par][:, _chunk(c)] = jnp.exp2(m_old - m_new)
        r.m[:, _chunk(c)] = m_new
        shift = m_new if mask is None else m_new - mask()
        r.p[par][:, _chunk(c)] = jnp.exp2(r.s[par][:, _chunk(c)] - shift).astype(BF16)


def _stage_c(r, par, blk, vt_rows, extras):
    vt = r.vt[0, 0, blk]
    for c in _live_chunks(extras):
        pv = jnp.dot(vt[vt_rows[c // N_SUB]], r.p[par][:, _chunk(c)], preferred_element_type=F32)
        r.acc[:, _chunk(c)] = r.al[par][:, _chunk(c)] * r.acc[:, _chunk(c)] + pv


def _live_chunks(extras):
    return [c for c in range(N_CHUNKS) if not isinstance(extras[c], str)]


def _visit_extras(bias_ref, offset_of, mask_of):
    extras = []
    for c in range(N_CHUNKS):
        g, s = divmod(c, N_SUB)
        off = offset_of(s)
        if off < 0:
            extras.append("future")
            continue
        bias = bias_ref[0, off, g] if off <= 1 else None
        extras.append((bias, mask_of(c) if off >= 1 else None))
    return extras


def _attend(r, tile, vt_rows, bias_ref, mask_row):
    r.m[...] = jnp.full(r.m.shape, NEG, F32)
    r.acc[...] = jnp.zeros(r.acc.shape, F32)
    base = N_SUB * tile
    far = 2

    def mask_of(blk):
        return (lambda c: None) if mask_row is None else (lambda c: (lambda: mask_row(blk, c)))

    def generic(blk):
        return _visit_extras(bias_ref, lambda s: far, mask_of(blk))

    def block_of(n):
        return jnp.where(n < N_SUB, base + n, jnp.where(n == N_SUB, base - 1, n - N_SUB - 1))

    visits = {}

    def stage(n, blk, extras):
        visits[n] = (blk, extras)
        _stage_a(r, n % 2, blk, extras)
        if n >= 1:
            _stage_b(r, (n - 1) % 2, visits[n - 1][1])
        if n >= 2:
            _stage_c(r, n % 2, visits[n - 2][0], vt_rows, visits[n - 2][1])

    for d in range(N_SUB):
        stage(d, base + d, _visit_extras(bias_ref, lambda s: s - d, mask_of(base + d)))

    @pl.when(tile == 0)
    def _():
        _stage_b(r, 1, visits[N_SUB - 1][1])
        _stage_c(r, 0, visits[N_SUB - 2][0], vt_rows, visits[N_SUB - 2][1])
        _stage_c(r, 1, visits[N_SUB - 1][0], vt_rows, visits[N_SUB - 1][1])

    @pl.when(tile > 0)
    def _():
        stage(N_SUB, base - 1,
              _visit_extras(bias_ref, lambda s: 1 if s == 0 else far, mask_of(base - 1)))
        stage(N_SUB + 1, 0, generic(0))

        def body(t, carry):
            n = N_SUB + 2 + 2 * t
            first = 1 + 2 * t
            _stage_a(r, 0, first, generic(first))
            _stage_b(r, 1, generic(first - 1))
            _stage_c(r, 0, block_of(n - 2), vt_rows, generic(0))
            _stage_a(r, 1, first + 1, generic(first + 1))
            _stage_b(r, 0, generic(first))
            _stage_c(r, 1, first - 1, vt_rows, generic(0))
            return carry

        lax.fori_loop(0, base // 2 - 1, body, 0)

        n_visits = base + N_SUB
        _stage_b(r, 1, generic(base - 2))
        _stage_c(r, 0, block_of(n_visits - 2), vt_rows, generic(0))
        _stage_c(r, 1, block_of(n_visits - 1), vt_rows, generic(0))


def _attn_scratch(acc_rows):
    return [
        pltpu.VMEM((PAIR, COLS), BF16),
        pltpu.VMEM((T, COLS_F32_PITCH), F32),
        pltpu.VMEM((T, COLS_F32_PITCH), F32),
        pltpu.VMEM((1, COLS), F32),
        pltpu.VMEM((1, COLS), F32),
        pltpu.VMEM((T, COLS), BF16),
        pltpu.VMEM((T, COLS), BF16),
        pltpu.VMEM((1, COLS), F32),
        pltpu.VMEM((1, COLS), F32),
        pltpu.VMEM((1, COLS), F32),
        pltpu.VMEM((acc_rows, COLS_F32_PITCH), F32),
    ]


def _make_refs(k_ref, vt_ref, scratch):
    q2, s0, s1, mb0, mb1, p0, p1, al0, al1, m, acc = scratch
    return _AttnRefs(k_ref, vt_ref, q2, (s0, s1), (mb0, mb1), (p0, p1), (al0, al1), m, acc)


def _attn_in_specs(nb, vt_rows):
    return [
        pl.BlockSpec((1, 1, N_QT, PAIR, TQS), lambda b, p, i: (b, p, i, 0, 0)),
        pl.BlockSpec((1, 1, nb, T, PAIR), lambda b, p, i: (b, p, 0, 0, 0)),
        pl.BlockSpec((1, 1, nb, vt_rows, T), lambda b, p, i: (b, p, 0, 0, 0)),
        pl.BlockSpec((1, 2, 2, T, T), lambda b, p, i: (p, 0, 0, 0, 0)),
    ]


def _diff_attn_kernel(qt_ref, k_ref, vt_ref, bias_ref, lam_ref, subln_ref, o_ref, *scratch):
    r = _make_refs(k_ref, vt_ref, scratch)
    _fill_q2(r.q2, qt_ref)
    _attend(r, pl.program_id(2), [slice(0, VT_ROWS_A)] * 2, bias_ref, None)

    lp = lam_ref[...]
    lam = (jnp.exp(jnp.sum(lp[0:1] * lp[1:2], axis=1, keepdims=True))
           - jnp.exp(jnp.sum(lp[2:3] * lp[3:4], axis=1, keepdims=True)) + LAM_INIT_A)
    o_all = r.acc[0:DV_A, 0:COLS] * (1.0 / r.acc[DV_A:DV_A + 1, 0:COLS])
    o = o_all[:, 0:TQ] - lam * o_all[:, TQ:COLS]
    o = o * lax.rsqrt(jnp.mean(o * o, axis=0, keepdims=True) + LN_EPS)
    o = o * subln_ref[...] * (1.0 - LAM_INIT_A)
    o_ref[0] = o.T.astype(BF16)


def _diff_attn(qt, k, vt, bias, lam_p, subln_col):
    bsz, _, nqs, _, _ = qt.shape
    nb = k.shape[2]
    return pl.pallas_call(
        _diff_attn_kernel,
        grid=(bsz, N_PAIRS, nqs // N_QT),
        in_specs=_attn_in_specs(nb, VT_ROWS_A) + [
            pl.BlockSpec((4, HEAD_DIM), lambda b, p, i: (0, 0)),
            pl.BlockSpec((PAIR, 1), lambda b, p, i: (0, 0)),
        ],
        out_specs=pl.BlockSpec((1, TQ, PAIR), lambda b, p, i: (b, i, p)),
        out_shape=jax.ShapeDtypeStruct((bsz, nqs * TQS, D_MODEL), BF16),
        scratch_shapes=_attn_scratch(VT_ROWS_A),
        compiler_params=_params(3),
        name="diff_attn",
    )(qt, k, vt, bias, lam_p, subln_col)


def _moba_attn_kernel(qt_ref, k_ref, vt_ref, bias_ref, mask_ref, o_ref, *scratch):
    r = _make_refs(k_ref, vt_ref, scratch)
    _fill_q2(r.q2, qt_ref)

    def mask_row(blk, c):
        g, s = divmod(c, N_SUB)
        t, s_in = divmod(s, TQS // T)
        col0 = g * TQS + s_in * T
        return mask_ref[0, 0, t, pl.ds(blk, 1), col0:col0 + T]

    rows = [slice(0, VT_ROWS_B), slice(VT_ROWS_B, 2 * VT_ROWS_B)]
    _attend(r, pl.program_id(2), rows, bias_ref, mask_row)

    o_all = r.acc[0:HEAD_DIM, 0:COLS] * (1.0 / r.acc[HEAD_DIM:HEAD_DIM + 1, 0:COLS])
    o = jnp.concatenate([o_all[:, 0:TQ], o_all[:, TQ:COLS]], axis=0)
    o_ref[0] = o.T.astype(BF16)


def _moba_attn(qt, k, vt, bias, mask):
    bsz, _, nqs, _, _ = qt.shape
    nb = k.shape[2]
    return pl.pallas_call(
        _moba_attn_kernel,
        grid=(bsz, N_PAIRS, nqs // N_QT),
        in_specs=_attn_in_specs(nb, 2 * VT_ROWS_B) + [
            pl.BlockSpec((1, 1, N_QT, nb, 2 * TQS), lambda b, p, i: (b, p, i, 0, 0)),
        ],
        out_specs=pl.BlockSpec((1, TQ, PAIR), lambda b, p, i: (b, i, p)),
        out_shape=jax.ShapeDtypeStruct((bsz, nqs * TQS, D_MODEL), BF16),
        scratch_shapes=_attn_scratch(VT_ROWS_B),
        compiler_params=_params(3),
        name="moba_attn",
    )(qt, k, vt, bias, mask)


def _post_attn_kernel(o_ref, h_ref, wo_ref, g1_ref, b1_ref, wup_ref, wdn_ref, g2_ref, b2_ref,
                      out_ref):
    y = ALPHA * h_ref[...] + jnp.dot(o_ref[...], wo_ref[...], preferred_element_type=F32)
    h = _layer_norm(y, g1_ref[...], b1_ref[...])
    hb = h.astype(BF16)
    y = ALPHA * h
    for c in range(D_FF // D_MODEL):
        cols = slice(c * D_MODEL, (c + 1) * D_MODEL)
        u = jnp.dot(hb, wup_ref[:, cols], preferred_element_type=F32)
        u = jnp.square(jnp.maximum(u, 0.0)).astype(BF16)
        y = y + jnp.dot(u, wdn_ref[cols, :], preferred_element_type=F32)
    out_ref[...] = _layer_norm(y, g2_ref[...], b2_ref[...])


def _post_attn(o, h, wo, g1, b1, wup, wdn, g2, b2):
    m = h.shape[0]
    tile = pl.BlockSpec((TM, D_MODEL), lambda i: (i, 0))
    return pl.pallas_call(
        _post_attn_kernel,
        grid=(m // TM,),
        in_specs=[
            tile,
            tile,
            _weight_spec((D_MODEL, D_MODEL)),
            _weight_spec((1, D_MODEL)),
            _weight_spec((1, D_MODEL)),
            _weight_spec((D_MODEL, D_FF)),
            _weight_spec((D_FF, D_MODEL)),
            _weight_spec((1, D_MODEL)),
            _weight_spec((1, D_MODEL)),
        ],
        out_specs=tile,
        out_shape=jax.ShapeDtypeStruct((m, D_MODEL), F32),
        compiler_params=_params(1),
        name="post_attn",
    )(o, h, wo, g1, b1, wup, wdn, g2, b2)


def kernel(x, w_in_a, lam_a, subln_a, w_out_a, w_kv_shared, w_q_b, w_out_b, rel_bias,
           ln1_g, ln1_b, ln2_g, ln2_b, w_up, w_down):
    bsz, seq, d = x.shape
    assert d == D_MODEL and seq % TM == 0 and TM % T == 0
    assert w_in_a.shape[0] == 1 and w_q_b.shape[0] == 1 and w_up.shape[0] == DEPTH
    m = bsz * seq
    row = lambda v: v.reshape(1, D_MODEL)

    bias = _bias_tiles(rel_bias)

    w_in = w_in_a[0]
    wqt = w_in[:, 0:D_MODEL].T.astype(BF16)
    wk = w_in[:, D_MODEL:2 * D_MODEL].astype(BF16)
    wvt = w_in[:, 2 * D_MODEL:].T.astype(BF16)
    qt, k, vt = _proj_a(x, wqt, wk, wvt)
    o = _diff_attn(qt, k, vt, bias, lam_a[0], subln_a[0].reshape(PAIR, 1))
    h = _post_attn(o.reshape(m, d), x.reshape(m, d), w_out_a[0].astype(BF16),
                   row(ln1_g[0]), row(ln1_b[0]), w_up[0].astype(BF16), w_down[0].astype(BF16),
                   row(ln2_g[0]), row(ln2_b[0]))

    h3 = h.reshape(bsz, seq, d)
    k2, vt2, kmean = _proj_kv(h3, w_kv_shared[:, 0:D_MODEL].astype(BF16),
                              w_kv_shared[:, D_MODEL:].T.astype(BF16))
    qt2, mask = _proj_q_gate(h3, w_q_b[0].T.astype(BF16), kmean)
    o = _moba_attn(qt2, k2, vt2, bias, mask)
    h = _post_attn(o.reshape(m, d), h, w_out_b[0].astype(BF16), row(ln1_g[1]), row(ln1_b[1]),
                   w_up[1].astype(BF16), w_down[1].astype(BF16), row(ln2_g[1]), row(ln2_b[1]))
    return h.reshape(bsz, seq, d)
```

```python
import functools
import math

import jax
import jax.numpy as jnp
from jax import lax
from jax.experimental import pallas as pl
from jax.experimental.pallas import tpu as pltpu

F32 = jnp.float32
BF16 = jnp.bfloat16

D_MODEL = 1024
D_FF = 4 * D_MODEL
DEPTH = 2
HEAD_DIM = 64
PAIR = 2 * HEAD_DIM
N_PAIRS = D_MODEL // PAIR
T = 256
TQS = 2 * T
N_SUB = 8
TQ = N_SUB * T
N_QT = TQ // TQS
COLS = 2 * TQ
N_CHUNKS = 2 * N_SUB
LANES = 128
COLS_F32_PITCH = COLS + 2 * LANES
BF16_ROWS = 16
MOBA_TOPK = 3
REL_BUCKETS = 32
REL_MAX_DIST = 128
LN_EPS = 1e-5
ALPHA = (2 * DEPTH) ** 0.25
LAM_INIT_A = 0.8 - 0.6 * math.exp(-0.3 * 0)
LOG2E = 1.0 / math.log(2.0)
Q_SCALE = HEAD_DIM ** -0.5 * LOG2E
NEG = -1e30
V7X_VMEM_LIMIT = 56 * 1024 * 1024
TM = TQS
DV_A = PAIR
VT_ROWS_A = DV_A + BF16_ROWS
VT_ROWS_B = HEAD_DIM + BF16_ROWS


def _params(n_axes):
    return pltpu.CompilerParams(
        dimension_semantics=("arbitrary",) * n_axes,
        vmem_limit_bytes=V7X_VMEM_LIMIT,
    )


def _nt_dot(a, b):
    return lax.dot_general(a, b, (((1,), (1,)), ((), ())), preferred_element_type=F32)


def _split_dot(a, b):
    a_hi = a.astype(BF16)
    a_lo = (a - a_hi.astype(F32)).astype(BF16)
    b_hi = b.astype(BF16)
    b_lo = (b - b_hi.astype(F32)).astype(BF16)
    dot = functools.partial(jnp.dot, preferred_element_type=F32)
    return dot(a_hi, b_hi) + (dot(a_hi, b_lo) + dot(a_lo, b_hi))


def _layer_norm(y, g, b):
    mu = jnp.mean(y, axis=-1, keepdims=True)
    yc = y - mu
    var = jnp.mean(yc * yc, axis=-1, keepdims=True)
    return yc * lax.rsqrt(var + LN_EPS) * g + b


def _chunk(c):
    return slice(c * T, (c + 1) * T)


def _bias_tile_kernel(tab_ref, out_ref):
    pr = pl.program_id(0)
    j = lax.broadcasted_iota(jnp.int32, (T, T), 0)
    i = lax.broadcasted_iota(jnp.int32, (T, T), 1)
    max_exact = REL_BUCKETS // 2
    for o in range(2):
        d = i - j + o * T
        n = jnp.maximum(d, 0)
        nf = jnp.maximum(n, max_exact).astype(F32)
        large = max_exact + (jnp.log(nf / max_exact) / math.log(REL_MAX_DIST / max_exact)
                             * (REL_BUCKETS - max_exact)).astype(jnp.int32)
        large = jnp.minimum(large, REL_BUCKETS - 1)
        bucket = jnp.where(n < max_exact, n, large)
        for m in range(2):
            col = 2 * pr + m
            far = tab_ref[REL_BUCKETS - 1, col]
            val = jnp.zeros((T, T), F32)
            for b in range(REL_BUCKETS - 1):
                val = jnp.where(bucket == b, (tab_ref[b, col] - far) * LOG2E, val)
            if o == 0:
                val = jnp.where(d >= 0, val, NEG)
            out_ref[0, o, m] = val


def _bias_tiles(rel_bias):
    return pl.pallas_call(
        _bias_tile_kernel,
        grid=(N_PAIRS,),
        in_specs=[pl.BlockSpec(memory_space=pltpu.SMEM)],
        out_specs=pl.BlockSpec((1, 2, 2, T, T), lambda p: (p, 0, 0, 0, 0)),
        out_shape=jax.ShapeDtypeStruct((N_PAIRS, 2, 2, T, T), F32),
        compiler_params=_params(1),
        name="bias_tiles",
    )(rel_bias)


def _store_q_tile(ref, qt):
    for p in range(N_PAIRS):
        ref[0, p, 0] = qt[p * PAIR:(p + 1) * PAIR].astype(BF16)


def _store_k_blocks(ref, k):
    for t in range(TM // T):
        for p in range(N_PAIRS):
            ref[0, p, t] = k[t * T:(t + 1) * T, p * PAIR:(p + 1) * PAIR].astype(BF16)


def _store_vt_blocks(ref, vt, width):
    groups = PAIR // width
    ones = jnp.ones((BF16_ROWS, T), BF16)
    for t in range(TM // T):
        for p in range(N_PAIRS):
            for g in range(groups):
                r0 = g * (width + BF16_ROWS)
                rows = slice(p * PAIR + g * width, p * PAIR + (g + 1) * width)
                ref[0, p, t, r0:r0 + width] = vt[rows, t * T:(t + 1) * T].astype(BF16)
                ref[0, p, t, r0 + width:r0 + width + BF16_ROWS] = ones


def _proj_a_kernel(x_ref, wqt_ref, wk_ref, wvt_ref, qt_ref, k_ref, vt_ref):
    xb = x_ref[0].astype(BF16)
    _store_q_tile(qt_ref, _nt_dot(wqt_ref[...], xb) * Q_SCALE)
    _store_k_blocks(k_ref, jnp.dot(xb, wk_ref[...], preferred_element_type=F32))
    _store_vt_blocks(vt_ref, _nt_dot(wvt_ref[...], xb), DV_A)


def _q_tile_spec():
    return pl.BlockSpec((1, N_PAIRS, 1, PAIR, TQS), lambda b, i: (b, 0, i, 0, 0))


def _k_blocks_spec():
    return pl.BlockSpec((1, N_PAIRS, TM // T, T, PAIR), lambda b, i: (b, 0, i, 0, 0))


def _vt_blocks_spec(rows):
    return pl.BlockSpec((1, N_PAIRS, TM // T, rows, T), lambda b, i: (b, 0, i, 0, 0))


def _weight_spec(shape):
    return pl.BlockSpec(shape, lambda *_: (0,) * len(shape), pipeline_mode=pl.Buffered(1))


def _proj_a(x, wqt, wk, wvt):
    bsz, seq, _ = x.shape
    nb = seq // T
    return pl.pallas_call(
        _proj_a_kernel,
        grid=(bsz, seq // TM),
        in_specs=[
            pl.BlockSpec((1, TM, D_MODEL), lambda b, i: (b, i, 0)),
            _weight_spec((D_MODEL, D_MODEL)),
            _weight_spec((D_MODEL, D_MODEL)),
            _weight_spec((D_MODEL, D_MODEL)),
        ],
        out_specs=[_q_tile_spec(), _k_blocks_spec(), _vt_blocks_spec(VT_ROWS_A)],
        out_shape=[
            jax.ShapeDtypeStruct((bsz, N_PAIRS, seq // TQS, PAIR, TQS), BF16),
            jax.ShapeDtypeStruct((bsz, N_PAIRS, nb, T, PAIR), BF16),
            jax.ShapeDtypeStruct((bsz, N_PAIRS, nb, VT_ROWS_A, T), BF16),
        ],
        compiler_params=_params(2),
        name="proj_a",
    )(x, wqt, wk, wvt)


def _proj_kv_kernel(h_ref, wk_ref, wvt_ref, k_ref, vt_ref, kmean_ref):
    hb = h_ref[0].astype(BF16)
    k = jnp.dot(hb, wk_ref[...], preferred_element_type=F32)
    _store_k_blocks(k_ref, k)
    for t in range(TM // T):
        blk = pl.program_id(1) * (TM // T) + t
        kmean_ref[0, pl.ds(blk, 1), :] = jnp.mean(k[t * T:(t + 1) * T], axis=0, keepdims=True)
    _store_vt_blocks(vt_ref, _nt_dot(wvt_ref[...], hb), HEAD_DIM)


def _proj_kv(h, wk, wvt):
    bsz, seq, _ = h.shape
    nb = seq // T
    return pl.pallas_call(
        _proj_kv_kernel,
        grid=(bsz, seq // TM),
        in_specs=[
            pl.BlockSpec((1, TM, D_MODEL), lambda b, i: (b, i, 0)),
            _weight_spec((D_MODEL, D_MODEL)),
            _weight_spec((D_MODEL, D_MODEL)),
        ],
        out_specs=[
            _k_blocks_spec(),
            _vt_blocks_spec(2 * VT_ROWS_B),
            pl.BlockSpec((1, nb, D_MODEL), lambda b, i: (b, 0, 0)),
        ],
        out_shape=[
            jax.ShapeDtypeStruct((bsz, N_PAIRS, nb, T, PAIR), BF16),
            jax.ShapeDtypeStruct((bsz, N_PAIRS, nb, 2 * VT_ROWS_B, T), BF16),
            jax.ShapeDtypeStruct((bsz, nb, D_MODEL), F32),
        ],
        compiler_params=_params(2),
        name="proj_kv",
    )(h, wk, wvt)


def _proj_q_gate_kernel(h_ref, wqt_ref, kmean_ref, qt_ref, mask_ref, *, n_blocks):
    hb = h_ref[0].astype(BF16)
    qt = _nt_dot(wqt_ref[...], hb)
    _store_q_tile(qt_ref, qt * Q_SCALE)

    km = kmean_ref[0]
    lane = lax.broadcasted_iota(jnp.int32, (n_blocks, PAIR), 1)
    blk = lax.broadcasted_iota(jnp.int32, (n_blocks, TM), 0)
    col = lax.broadcasted_iota(jnp.int32, (1, TM), 1)
    q_blk = pl.program_id(1) * (TM // T) + lax.shift_right_logical(col, T.bit_length() - 1)
    past = blk < q_blk
    blk_f = blk.astype(F32)
    for p in range(N_PAIRS):
        km_p = km[:, p * PAIR:(p + 1) * PAIR]
        km2 = jnp.concatenate(
            [jnp.where(lane < HEAD_DIM, km_p, 0.0), jnp.where(lane >= HEAD_DIM, km_p, 0.0)],
            axis=0)
        g2 = _split_dot(km2, qt[p * PAIR:(p + 1) * PAIR])
        for sub in range(2):
            g = jnp.where(past, g2[sub * n_blocks:(sub + 1) * n_blocks], -jnp.inf)
            keep = jnp.zeros((n_blocks, TM), F32)
            for _ in range(MOBA_TOPK):
                top = jnp.max(g, axis=0, keepdims=True)
                first = jnp.min(jnp.where(g == top, blk_f, float(n_blocks)), axis=0, keepdims=True)
                pick = (blk_f == first) & (top > -jnp.inf)
                keep = jnp.where(pick, 1.0, keep)
                g = jnp.where(pick, -jnp.inf, g)
            mask_ref[0, p, 0, :, sub * TQS:(sub + 1) * TQS] = jnp.where(keep > 0.0, 0.0, NEG)


def _proj_q_gate(h, wqt, kmean):
    bsz, seq, _ = h.shape
    nb = seq // T
    return pl.pallas_call(
        functools.partial(_proj_q_gate_kernel, n_blocks=nb),
        grid=(bsz, seq // TM),
        in_specs=[
            pl.BlockSpec((1, TM, D_MODEL), lambda b, i: (b, i, 0)),
            _weight_spec((D_MODEL, D_MODEL)),
            pl.BlockSpec((1, nb, D_MODEL), lambda b, i: (b, 0, 0)),
        ],
        out_specs=[
            _q_tile_spec(),
            pl.BlockSpec((1, N_PAIRS, 1, nb, 2 * TQS), lambda b, i: (b, 0, i, 0, 0)),
        ],
        out_shape=[
            jax.ShapeDtypeStruct((bsz, N_PAIRS, seq // TQS, PAIR, TQS), BF16),
            jax.ShapeDtypeStruct((bsz, N_PAIRS, seq // TQS, nb, 2 * TQS), F32),
        ],
        compiler_params=_params(2),
        name="proj_q_gate",
    )(h, wqt, kmean)


class _AttnRefs:
    def __init__(self, k_ref, vt_ref, q2, s, mb, p, al, m, acc):
        self.k, self.vt, self.q2 = k_ref, vt_ref, q2
        self.s, self.mb, self.p, self.al, self.m, self.acc = s, mb, p, al, m, acc


def _fill_q2(q2_ref, qt_ref):
    zeros = jnp.zeros((HEAD_DIM, TQ), BF16)
    q2_ref[0:HEAD_DIM, TQ:COLS] = zeros
    q2_ref[HEAD_DIM:PAIR, 0:TQ] = zeros
    for t in range(N_QT):
        qt = qt_ref[0, 0, t]
        q2_ref[0:HEAD_DIM, t * TQS:(t + 1) * TQS] = qt[0:HEAD_DIM]
        q2_ref[HEAD_DIM:PAIR, TQ + t * TQS:TQ + (t + 1) * TQS] = qt[HEAD_DIM:PAIR]


def _stage_a(r, par, blk, extras):
    kj = r.k[0, 0, blk]
    for c in range(N_CHUNKS):
        if isinstance(extras[c], str):
            continue
        bias, mask = extras[c]
        s = jnp.dot(kj, r.q2[:, _chunk(c)], preferred_element_type=F32)
        if bias is not None:
            s = s + bias
        r.s[par][:, _chunk(c)] = s
        col_max = jnp.max(s, axis=0, keepdims=True)
        r.mb[par][:, _chunk(c)] = col_max if mask is None else col_max + mask()


def _stage_b(r, par, extras):
    for c in _live_chunks(extras):
        mask = extras[c][1]
        m_old = r.m[:, _chunk(c)]
        m_new = jnp.maximum(m_old, r.mb[par][:, _chunk(c)])
        r.al[par][:, _chunk(c)] = jnp.exp2(m_old - m_new)
        r.m[:, _chunk(c)] = m_new
        shift = m_new if mask is None else m_new - mask()
        r.p[par][:, _chunk(c)] = jnp.exp2(r.s[par][:, _chunk(c)] - shift).astype(BF16)


def _stage_c(r, par, blk, vt_rows, extras):
    vt = r.vt[0, 0, blk]
    for c in _live_chunks(extras):
        pv = jnp.dot(vt[vt_rows[c // N_SUB]], r.p[par][:, _chunk(c)], preferred_element_type=F32)
        r.acc[:, _chunk(c)] = r.al[par][:, _chunk(c)] * r.acc[:, _chunk(c)] + pv


def _stage_c_pair(r, blk0, blk1, vt_rows):
    vt0 = r.vt[0, 0, blk0]
    vt1 = r.vt[0, 0, blk1]
    for c in range(N_CHUNKS):
        rows = vt_rows[c // N_SUB]
        pv0 = jnp.dot(vt0[rows], r.p[0][:, _chunk(c)], preferred_element_type=F32)
        pv1 = jnp.dot(vt1[rows], r.p[1][:, _chunk(c)], preferred_element_type=F32)
        acc = r.al[0][:, _chunk(c)] * r.acc[:, _chunk(c)] + pv0
        r.acc[:, _chunk(c)] = r.al[1][:, _chunk(c)] * acc + pv1


def _live_chunks(extras):
    return [c for c in range(N_CHUNKS) if not isinstance(extras[c], str)]


def _visit_extras(bias_ref, offset_of, mask_of):
    extras = []
    for c in range(N_CHUNKS):
        g, s = divmod(c, N_SUB)
        off = offset_of(s)
        if off < 0:
            extras.append("future")
            continue
        bias = bias_ref[0, off, g] if off <= 1 else None
        extras.append((bias, mask_of(c) if off >= 1 else None))
    return extras


def _attend(r, tile, vt_rows, bias_ref, mask_row):
    r.m[...] = jnp.full(r.m.shape, NEG, F32)
    r.acc[...] = jnp.zeros(r.acc.shape, F32)
    base = N_SUB * tile
    far = 2

    def mask_of(blk):
        return (lambda c: None) if mask_row is None else (lambda c: (lambda: mask_row(blk, c)))

    def generic(blk):
        return _visit_extras(bias_ref, lambda s: far, mask_of(blk))

    def block_of(n):
        return jnp.where(n < N_SUB, base + n, jnp.where(n == N_SUB, base - 1, n - N_SUB - 1))

    visits = {}

    def stage(n, blk, extras):
        visits[n] = (blk, extras)
        _stage_a(r, n % 2, blk, extras)
        if n >= 1:
            _stage_b(r, (n - 1) % 2, visits[n - 1][1])
        if n >= 2:
            _stage_c(r, n % 2, visits[n - 2][0], vt_rows, visits[n - 2][1])

    for d in range(N_SUB):
        stage(d, base + d, _visit_extras(bias_ref, lambda s: s - d, mask_of(base + d)))

    @pl.when(tile == 0)
    def _():
        _stage_b(r, 1, visits[N_SUB - 1][1])
        _stage_c(r, 0, visits[N_SUB - 2][0], vt_rows, visits[N_SUB - 2][1])
        _stage_c(r, 1, visits[N_SUB - 1][0], vt_rows, visits[N_SUB - 1][1])

    @pl.when(tile > 0)
    def _():
        stage(N_SUB, base - 1,
              _visit_extras(bias_ref, lambda s: 1 if s == 0 else far, mask_of(base - 1)))
        stage(N_SUB + 1, 0, generic(0))

        def body(t, carry):
            n = N_SUB + 2 + 2 * t
            first = 1 + 2 * t
            _stage_a(r, 0, first, generic(first))
            _stage_b(r, 1, generic(first - 1))
            _stage_c_pair(r, block_of(n - 2), first - 1, vt_rows)
            _stage_a(r, 1, first + 1, generic(first + 1))
            _stage_b(r, 0, generic(first))
            return carry

        lax.fori_loop(0, base // 2 - 1, body, 0)

        n_visits = base + N_SUB
        _stage_b(r, 1, generic(base - 2))
        _stage_c(r, 0, block_of(n_visits - 2), vt_rows, generic(0))
        _stage_c(r, 1, block_of(n_visits - 1), vt_rows, generic(0))


def _attn_scratch(acc_rows):
    return [
        pltpu.VMEM((PAIR, COLS), BF16),
        pltpu.VMEM((T, COLS_F32_PITCH), F32),
        pltpu.VMEM((T, COLS_F32_PITCH), F32),
        pltpu.VMEM((1, COLS), F32),
        pltpu.VMEM((1, COLS), F32),
        pltpu.VMEM((T, COLS), BF16),
        pltpu.VMEM((T, COLS), BF16),
        pltpu.VMEM((1, COLS), F32),
        pltpu.VMEM((1, COLS), F32),
        pltpu.VMEM((1, COLS), F32),
        pltpu.VMEM((acc_rows, COLS_F32_PITCH), F32),
    ]


def _make_refs(k_ref, vt_ref, scratch):
    q2, s0, s1, mb0, mb1, p0, p1, al0, al1, m, acc = scratch
    return _AttnRefs(k_ref, vt_ref, q2, (s0, s1), (mb0, mb1), (p0, p1), (al0, al1), m, acc)


def _attn_in_specs(nb, vt_rows):
    return [
        pl.BlockSpec((1, 1, N_QT, PAIR, TQS), lambda b, p, i: (b, p, i, 0, 0)),
        pl.BlockSpec((1, 1, nb, T, PAIR), lambda b, p, i: (b, p, 0, 0, 0)),
        pl.BlockSpec((1, 1, nb, vt_rows, T), lambda b, p, i: (b, p, 0, 0, 0)),
        pl.BlockSpec((1, 2, 2, T, T), lambda b, p, i: (p, 0, 0, 0, 0)),
    ]


def _diff_attn_kernel(qt_ref, k_ref, vt_ref, bias_ref, lam_ref, subln_ref, o_ref, *scratch):
    r = _make_refs(k_ref, vt_ref, scratch)
    _fill_q2(r.q2, qt_ref)
    _attend(r, pl.program_id(2), [slice(0, VT_ROWS_A)] * 2, bias_ref, None)

    lp = lam_ref[...]
    lam = (jnp.exp(jnp.sum(lp[0:1] * lp[1:2], axis=1, keepdims=True))
           - jnp.exp(jnp.sum(lp[2:3] * lp[3:4], axis=1, keepdims=True)) + LAM_INIT_A)
    o_all = r.acc[0:DV_A, 0:COLS] * (1.0 / r.acc[DV_A:DV_A + 1, 0:COLS])
    o = o_all[:, 0:TQ] - lam * o_all[:, TQ:COLS]
    o = o * lax.rsqrt(jnp.mean(o * o, axis=0, keepdims=True) + LN_EPS)
    o = o * subln_ref[...] * (1.0 - LAM_INIT_A)
    o_ref[0] = o.T.astype(BF16)


def _diff_attn(qt, k, vt, bias, lam_p, subln_col):
    bsz, _, nqs, _, _ = qt.shape
    nb = k.shape[2]
    return pl.pallas_call(
        _diff_attn_kernel,
        grid=(bsz, N_PAIRS, nqs // N_QT),
        in_specs=_attn_in_specs(nb, VT_ROWS_A) + [
            pl.BlockSpec((4, HEAD_DIM), lambda b, p, i: (0, 0)),
            pl.BlockSpec((PAIR, 1), lambda b, p, i: (0, 0)),
        ],
        out_specs=pl.BlockSpec((1, TQ, PAIR), lambda b, p, i: (b, i, p)),
        out_shape=jax.ShapeDtypeStruct((bsz, nqs * TQS, D_MODEL), BF16),
        scratch_shapes=_attn_scratch(VT_ROWS_A),
        compiler_params=_params(3),
        name="diff_attn",
    )(qt, k, vt, bias, lam_p, subln_col)


def _moba_attn_kernel(qt_ref, k_ref, vt_ref, bias_ref, mask_ref, o_ref, *scratch):
    r = _make_refs(k_ref, vt_ref, scratch)
    _fill_q2(r.q2, qt_ref)

    def mask_row(blk, c):
        g, s = divmod(c, N_SUB)
        t, s_in = divmod(s, TQS // T)
        col0 = g * TQS + s_in * T
        return mask_ref[0, 0, t, pl.ds(blk, 1), col0:col0 + T]

    rows = [slice(0, VT_ROWS_B), slice(VT_ROWS_B, 2 * VT_ROWS_B)]
    _attend(r, pl.program_id(2), rows, bias_ref, mask_row)

    o_all = r.acc[0:HEAD_DIM, 0:COLS] * (1.0 / r.acc[HEAD_DIM:HEAD_DIM + 1, 0:COLS])
    o = jnp.concatenate([o_all[:, 0:TQ], o_all[:, TQ:COLS]], axis=0)
    o_ref[0] = o.T.astype(BF16)


def _moba_attn(qt, k, vt, bias, mask):
    bsz, _, nqs, _, _ = qt.shape
    nb = k.shape[2]
    return pl.pallas_call(
        _moba_attn_kernel,
        grid=(bsz, N_PAIRS, nqs // N_QT),
        in_specs=_attn_in_specs(nb, 2 * VT_ROWS_B) + [
            pl.BlockSpec((1, 1, N_QT, nb, 2 * TQS), lambda b, p, i: (b, p, i, 0, 0)),
        ],
        out_specs=pl.BlockSpec((1, TQ, PAIR), lambda b, p, i: (b, i, p)),
        out_shape=jax.ShapeDtypeStruct((bsz, nqs * TQS, D_MODEL), BF16),
        scratch_shapes=_attn_scratch(VT_ROWS_B),
        compiler_params=_params(3),
        name="moba_attn",
    )(qt, k, vt, bias, mask)


def _post_attn_kernel(o_ref, h_ref, wo_ref, g1_ref, b1_ref, wup_ref, wdn_ref, g2_ref, b2_ref,
                      out_ref):
    y = ALPHA * h_ref[...] + jnp.dot(o_ref[...], wo_ref[...], preferred_element_type=F32)
    h = _layer_norm(y, g1_ref[...], b1_ref[...])
    hb = h.astype(BF16)
    y = ALPHA * h
    for c in range(D_FF // D_MODEL):
        cols = slice(c * D_MODEL, (c + 1) * D_MODEL)
        u = jnp.dot(hb, wup_ref[:, cols], preferred_element_type=F32)
        u = jnp.square(jnp.maximum(u, 0.0)).astype(BF16)
        y = y + jnp.dot(u, wdn_ref[cols, :], preferred_element_type=F32)
    out_ref[...] = _layer_norm(y, g2_ref[...], b2_ref[...])


def _post_attn(o, h, wo, g1, b1, wup, wdn, g2, b2):
    m = h.shape[0]
    tile = pl.BlockSpec((TM, D_MODEL), lambda i: (i, 0))
    return pl.pallas_call(
        _post_attn_kernel,
        grid=(m // TM,),
        in_specs=[
            tile,
            tile,
            _weight_spec((D_MODEL, D_MODEL)),
            _weight_spec((1, D_MODEL)),
            _weight_spec((1, D_MODEL)),
            _weight_spec((D_MODEL, D_FF)),
            _weight_spec((D_FF, D_MODEL)),
            _weight_spec((1, D_MODEL)),
            _weight_spec((1, D_MODEL)),
        ],
        out_specs=tile,
        out_shape=jax.ShapeDtypeStruct((m, D_MODEL), F32),
        compiler_params=_params(1),
        name="post_attn",
    )(o, h, wo, g1, b1, wup, wdn, g2, b2)


def kernel(x, w_in_a, lam_a, subln_a, w_out_a, w_kv_shared, w_q_b, w_out_b, rel_bias,
           ln1_g, ln1_b, ln2_g, ln2_b, w_up, w_down):
    bsz, seq, d = x.shape
    assert d == D_MODEL and seq % TM == 0 and TM % T == 0
    assert w_in_a.shape[0] == 1 and w_q_b.shape[0] == 1 and w_up.shape[0] == DEPTH
    m = bsz * seq
    row = lambda v: v.reshape(1, D_MODEL)

    bias = _bias_tiles(rel_bias)

    w_in = w_in_a[0]
    wqt = w_in[:, 0:D_MODEL].T.astype(BF16)
    wk = w_in[:, D_MODEL:2 * D_MODEL].astype(BF16)
    wvt = w_in[:, 2 * D_MODEL:].T.astype(BF16)
    qt, k, vt = _proj_a(x, wqt, wk, wvt)
    o = _diff_attn(qt, k, vt, bias, lam_a[0], subln_a[0].reshape(PAIR, 1))
    h = _post_attn(o.reshape(m, d), x.reshape(m, d), w_out_a[0].astype(BF16),
                   row(ln1_g[0]), row(ln1_b[0]), w_up[0].astype(BF16), w_down[0].astype(BF16),
                   row(ln2_g[0]), row(ln2_b[0]))

    h3 = h.reshape(bsz, seq, d)
    k2, vt2, kmean = _proj_kv(h3, w_kv_shared[:, 0:D_MODEL].astype(BF16),
                              w_kv_shared[:, D_MODEL:].T.astype(BF16))
    qt2, mask = _proj_q_gate(h3, w_q_b[0].T.astype(BF16), kmean)
    o = _moba_attn(qt2, k2, vt2, bias, mask)
    h = _post_attn(o.reshape(m, d), h, w_out_b[0].astype(BF16), row(ln1_g[1]), row(ln1_b[1]),
                   w_up[1].astype(BF16), w_down[1].astype(BF16), row(ln2_g[1]), row(ln2_b[1]))
    return h.reshape(bsz, seq, d)
```

```python
import functools
import math

import jax
import jax.numpy as jnp
from jax import lax
from jax.experimental import pallas as pl
from jax.experimental.pallas import tpu as pltpu

F32 = jnp.float32
BF16 = jnp.bfloat16

D_MODEL = 1024
D_FF = 4 * D_MODEL
DEPTH = 2
HEAD_DIM = 64
PAIR = 2 * HEAD_DIM
N_PAIRS = D_MODEL // PAIR
T = 256
TQS = 2 * T
N_SUB = 8
TQ = N_SUB * T
N_QT = TQ // TQS
COLS = 2 * TQ
N_CHUNKS = 2 * N_SUB
LANES = 128
COLS_F32_PITCH = COLS + 2 * LANES
BF16_ROWS = 16
MOBA_TOPK = 3
REL_BUCKETS = 32
REL_MAX_DIST = 128
LN_EPS = 1e-5
ALPHA = (2 * DEPTH) ** 0.25
LAM_INIT_A = 0.8 - 0.6 * math.exp(-0.3 * 0)
LOG2E = 1.0 / math.log(2.0)
Q_SCALE = HEAD_DIM ** -0.5 * LOG2E
NEG = -1e30
V7X_VMEM_LIMIT = 56 * 1024 * 1024
TM = TQS
DV_A = PAIR
VT_ROWS_A = DV_A + BF16_ROWS
VT_ROWS_B = HEAD_DIM + BF16_ROWS


def _params(n_axes):
    return pltpu.CompilerParams(
        dimension_semantics=("arbitrary",) * n_axes,
        vmem_limit_bytes=V7X_VMEM_LIMIT,
    )


def _nt_dot(a, b):
    return lax.dot_general(a, b, (((1,), (1,)), ((), ())), preferred_element_type=F32)


def _split_dot(a, b):
    a_hi = a.astype(BF16)
    a_lo = (a - a_hi.astype(F32)).astype(BF16)
    b_hi = b.astype(BF16)
    b_lo = (b - b_hi.astype(F32)).astype(BF16)
    dot = functools.partial(jnp.dot, preferred_element_type=F32)
    return dot(a_hi, b_hi) + (dot(a_hi, b_lo) + dot(a_lo, b_hi))


def _layer_norm(y, g, b):
    mu = jnp.mean(y, axis=-1, keepdims=True)
    yc = y - mu
    var = jnp.mean(yc * yc, axis=-1, keepdims=True)
    return yc * lax.rsqrt(var + LN_EPS) * g + b


def _chunk(c):
    return slice(c * T, (c + 1) * T)


def _bias_tile_kernel(tab_ref, out_ref):
    pr = pl.program_id(0)
    j = lax.broadcasted_iota(jnp.int32, (T, T), 0)
    i = lax.broadcasted_iota(jnp.int32, (T, T), 1)
    max_exact = REL_BUCKETS // 2
    for o in range(2):
        d = i - j + o * T
        n = jnp.maximum(d, 0)
        nf = jnp.maximum(n, max_exact).astype(F32)
        large = max_exact + (jnp.log(nf / max_exact) / math.log(REL_MAX_DIST / max_exact)
                             * (REL_BUCKETS - max_exact)).astype(jnp.int32)
        large = jnp.minimum(large, REL_BUCKETS - 1)
        bucket = jnp.where(n < max_exact, n, large)
        for m in range(2):
            col = 2 * pr + m
            far = tab_ref[REL_BUCKETS - 1, col]
            val = jnp.zeros((T, T), F32)
            for b in range(REL_BUCKETS - 1):
                val = jnp.where(bucket == b, (tab_ref[b, col] - far) * LOG2E, val)
            if o == 0:
                val = jnp.where(d >= 0, val, NEG)
            out_ref[0, o, m] = val


def _bias_tiles(rel_bias):
    return pl.pallas_call(
        _bias_tile_kernel,
        grid=(N_PAIRS,),
        in_specs=[pl.BlockSpec(memory_space=pltpu.SMEM)],
        out_specs=pl.BlockSpec((1, 2, 2, T, T), lambda p: (p, 0, 0, 0, 0)),
        out_shape=jax.ShapeDtypeStruct((N_PAIRS, 2, 2, T, T), F32),
        compiler_params=_params(1),
        name="bias_tiles",
    )(rel_bias)


def _store_q_tile(ref, qt):
    for p in range(N_PAIRS):
        ref[0, p, 0] = qt[p * PAIR:(p + 1) * PAIR].astype(BF16)


def _store_k_blocks(ref, k):
    for t in range(TM // T):
        for p in range(N_PAIRS):
            ref[0, p, t] = k[t * T:(t + 1) * T, p * PAIR:(p + 1) * PAIR].astype(BF16)


def _store_vt_blocks(ref, vt, width):
    groups = PAIR // width
    ones = jnp.ones((BF16_ROWS, T), BF16)
    for t in range(TM // T):
        for p in range(N_PAIRS):
            for g in range(groups):
                r0 = g * (width + BF16_ROWS)
                rows = slice(p * PAIR + g * width, p * PAIR + (g + 1) * width)
                ref[0, p, t, r0:r0 + width] = vt[rows, t * T:(t + 1) * T].astype(BF16)
                ref[0, p, t, r0 + width:r0 + width + BF16_ROWS] = ones


def _proj_a_kernel(x_ref, wqt_ref, wk_ref, wvt_ref, qt_ref, k_ref, vt_ref):
    xb = x_ref[0].astype(BF16)
    _store_q_tile(qt_ref, _nt_dot(wqt_ref[...], xb) * Q_SCALE)
    _store_k_blocks(k_ref, jnp.dot(xb, wk_ref[...], preferred_element_type=F32))
    _store_vt_blocks(vt_ref, _nt_dot(wvt_ref[...], xb), DV_A)


def _q_tile_spec():
    return pl.BlockSpec((1, N_PAIRS, 1, PAIR, TQS), lambda b, i: (b, 0, i, 0, 0))


def _k_blocks_spec():
    return pl.BlockSpec((1, N_PAIRS, TM // T, T, PAIR), lambda b, i: (b, 0, i, 0, 0))


def _vt_blocks_spec(rows):
    return pl.BlockSpec((1, N_PAIRS, TM // T, rows, T), lambda b, i: (b, 0, i, 0, 0))


def _weight_spec(shape):
    return pl.BlockSpec(shape, lambda *_: (0,) * len(shape), pipeline_mode=pl.Buffered(1))


def _proj_a(x, wqt, wk, wvt):
    bsz, seq, _ = x.shape
    nb = seq // T
    return pl.pallas_call(
        _proj_a_kernel,
        grid=(bsz, seq // TM),
        in_specs=[
            pl.BlockSpec((1, TM, D_MODEL), lambda b, i: (b, i, 0)),
            _weight_spec((D_MODEL, D_MODEL)),
            _weight_spec((D_MODEL, D_MODEL)),
            _weight_spec((D_MODEL, D_MODEL)),
        ],
        out_specs=[_q_tile_spec(), _k_blocks_spec(), _vt_blocks_spec(VT_ROWS_A)],
        out_shape=[
            jax.ShapeDtypeStruct((bsz, N_PAIRS, seq // TQS, PAIR, TQS), BF16),
            jax.ShapeDtypeStruct((bsz, N_PAIRS, nb, T, PAIR), BF16),
            jax.ShapeDtypeStruct((bsz, N_PAIRS, nb, VT_ROWS_A, T), BF16),
        ],
        compiler_params=_params(2),
        name="proj_a",
    )(x, wqt, wk, wvt)


def _proj_kv_kernel(h_ref, wk_ref, wvt_ref, k_ref, vt_ref, kmean_ref):
    hb = h_ref[0].astype(BF16)
    k = jnp.dot(hb, wk_ref[...], preferred_element_type=F32)
    _store_k_blocks(k_ref, k)
    for t in range(TM // T):
        blk = pl.program_id(1) * (TM // T) + t
        kmean_ref[0, pl.ds(blk, 1), :] = jnp.mean(k[t * T:(t + 1) * T], axis=0, keepdims=True)
    _store_vt_blocks(vt_ref, _nt_dot(wvt_ref[...], hb), HEAD_DIM)


def _proj_kv(h, wk, wvt):
    bsz, seq, _ = h.shape
    nb = seq // T
    return pl.pallas_call(
        _proj_kv_kernel,
        grid=(bsz, seq // TM),
        in_specs=[
            pl.BlockSpec((1, TM, D_MODEL), lambda b, i: (b, i, 0)),
            _weight_spec((D_MODEL, D_MODEL)),
            _weight_spec((D_MODEL, D_MODEL)),
        ],
        out_specs=[
            _k_blocks_spec(),
            _vt_blocks_spec(2 * VT_ROWS_B),
            pl.BlockSpec((1, nb, D_MODEL), lambda b, i: (b, 0, 0)),
        ],
        out_shape=[
            jax.ShapeDtypeStruct((bsz, N_PAIRS, nb, T, PAIR), BF16),
            jax.ShapeDtypeStruct((bsz, N_PAIRS, nb, 2 * VT_ROWS_B, T), BF16),
            jax.ShapeDtypeStruct((bsz, nb, D_MODEL), F32),
        ],
        compiler_params=_params(2),
        name="proj_kv",
    )(h, wk, wvt)


def _proj_q_gate_kernel(h_ref, wqt_ref, kmean_ref, qt_ref, mask_ref, *, n_blocks):
    hb = h_ref[0].astype(BF16)
    qt = _nt_dot(wqt_ref[...], hb)
    _store_q_tile(qt_ref, qt * Q_SCALE)

    km = kmean_ref[0]
    lane = lax.broadcasted_iota(jnp.int32, (n_blocks, PAIR), 1)
    blk = lax.broadcasted_iota(jnp.int32, (n_blocks, TM), 0)
    col = lax.broadcasted_iota(jnp.int32, (1, TM), 1)
    q_blk = pl.program_id(1) * (TM // T) + lax.shift_right_logical(col, T.bit_length() - 1)
    past = blk < q_blk
    blk_f = blk.astype(F32)
    for p in range(N_PAIRS):
        km_p = km[:, p * PAIR:(p + 1) * PAIR]
        km2 = jnp.concatenate(
            [jnp.where(lane < HEAD_DIM, km_p, 0.0), jnp.where(lane >= HEAD_DIM, km_p, 0.0)],
            axis=0)
        g2 = _split_dot(km2, qt[p * PAIR:(p + 1) * PAIR])
        for sub in range(2):
            g = jnp.where(past, g2[sub * n_blocks:(sub + 1) * n_blocks], -jnp.inf)
            keep = jnp.zeros((n_blocks, TM), F32)
            for _ in range(MOBA_TOPK):
                top = jnp.max(g, axis=0, keepdims=True)
                first = jnp.min(jnp.where(g == top, blk_f, float(n_blocks)), axis=0, keepdims=True)
                pick = (blk_f == first) & (top > -jnp.inf)
                keep = jnp.where(pick, 1.0, keep)
                g = jnp.where(pick, -jnp.inf, g)
            mask_ref[0, p, 0, :, sub * TQS:(sub + 1) * TQS] = jnp.where(keep > 0.0, 0.0, NEG)


def _proj_q_gate(h, wqt, kmean):
    bsz, seq, _ = h.shape
    nb = seq // T
    return pl.pallas_call(
        functools.partial(_proj_q_gate_kernel, n_blocks=nb),
        grid=(bsz, seq // TM),
        in_specs=[
            pl.BlockSpec((1, TM, D_MODEL), lambda b, i: (b, i, 0)),
            _weight_spec((D_MODEL, D_MODEL)),
            pl.BlockSpec((1, nb, D_MODEL), lambda b, i: (b, 0, 0)),
        ],
        out_specs=[
            _q_tile_spec(),
            pl.BlockSpec((1, N_PAIRS, 1, nb, 2 * TQS), lambda b, i: (b, 0, i, 0, 0)),
        ],
        out_shape=[
            jax.ShapeDtypeStruct((bsz, N_PAIRS, seq // TQS, PAIR, TQS), BF16),
            jax.ShapeDtypeStruct((bsz, N_PAIRS, seq // TQS, nb, 2 * TQS), F32),
        ],
        compiler_params=_params(2),
        name="proj_q_gate",
    )(h, wqt, kmean)


class _AttnRefs:
    def __init__(self, k_ref, vt_ref, q2, s, mb, p, al, m, acc):
        self.k, self.vt, self.q2 = k_ref, vt_ref, q2
        self.s, self.mb, self.p, self.al, self.m, self.acc = s, mb, p, al, m, acc


def _fill_q2(q2_ref, qt_ref):
    zeros = jnp.zeros((HEAD_DIM, TQ), BF16)
    q2_ref[0:HEAD_DIM, TQ:COLS] = zeros
    q2_ref[HEAD_DIM:PAIR, 0:TQ] = zeros
    for t in range(N_QT):
        qt = qt_ref[0, 0, t]
        q2_ref[0:HEAD_DIM, t * TQS:(t + 1) * TQS] = qt[0:HEAD_DIM]
        q2_ref[HEAD_DIM:PAIR, TQ + t * TQS:TQ + (t + 1) * TQS] = qt[HEAD_DIM:PAIR]


def _stage_a(r, par, blk, extras):
    kj = r.k[0, 0, blk]
    for c in range(N_CHUNKS):
        if isinstance(extras[c], str):
            continue
        bias, mask = extras[c]
        s = jnp.dot(kj, r.q2[:, _chunk(c)], preferred_element_type=F32)
        if bias is not None:
            s = s + bias
        r.s[par][:, _chunk(c)] = s
        col_max = jnp.max(s, axis=0, keepdims=True)
        r.mb[par][:, _chunk(c)] = col_max if mask is None else col_max + mask()


def _stage_b(r, par, extras):
    for c in _live_chunks(extras):
        mask = extras[c][1]
        m_old = r.m[:, _chunk(c)]
        m_new = jnp.maximum(m_old, r.mb[par][:, _chunk(c)])
        r.al[par][:, _chunk(c)] = jnp.exp2(m_old - m_new)
        r.m[:, _chunk(c)] = m_new
        shift = m_new if mask is None else m_new - mask()
        r.p[par][:, _chunk(c)] = jnp.exp2(r.s[par][:, _chunk(c)] - shift).astype(BF16)


def _stage_c(r, par, blk, vt_rows, extras):
    vt = r.vt[0, 0, blk]
    for c in _live_chunks(extras):
        pv = jnp.dot(vt[vt_rows[c // N_SUB]], r.p[par][:, _chunk(c)], preferred_element_type=F32)
        r.acc[:, _chunk(c)] = r.al[par][:, _chunk(c)] * r.acc[:, _chunk(c)] + pv


def _wide(u):
    return slice(u * TQS, (u + 1) * TQS)


def _stage_a_wide(r, par, blk, mask_wide):
    kj = r.k[0, 0, blk]
    for u in range(COLS // TQS):
        s = jnp.dot(kj, r.q2[:, _wide(u)], preferred_element_type=F32)
        r.s[par][:, _wide(u)] = s
        col_max = jnp.max(s, axis=0, keepdims=True)
        r.mb[par][:, _wide(u)] = col_max if mask_wide is None else col_max + mask_wide(blk, u)


def _stage_b_wide(r, par, blk, mask_wide):
    for u in range(COLS // TQS):
        m_old = r.m[:, _wide(u)]
        m_new = jnp.maximum(m_old, r.mb[par][:, _wide(u)])
        r.al[par][:, _wide(u)] = jnp.exp2(m_old - m_new)
        r.m[:, _wide(u)] = m_new
        shift = m_new if mask_wide is None else m_new - mask_wide(blk, u)
        r.p[par][:, _wide(u)] = jnp.exp2(r.s[par][:, _wide(u)] - shift).astype(BF16)


def _stage_c_wide(r, par, blk, vt_rows):
    vt = r.vt[0, 0, blk]
    for u in range(COLS // TQS):
        pv = jnp.dot(vt[vt_rows[u // N_QT]], r.p[par][:, _wide(u)], preferred_element_type=F32)
        r.acc[:, _wide(u)] = r.al[par][:, _wide(u)] * r.acc[:, _wide(u)] + pv


def _live_chunks(extras):
    return [c for c in range(N_CHUNKS) if not isinstance(extras[c], str)]


def _visit_extras(bias_ref, offset_of, mask_of):
    extras = []
    for c in range(N_CHUNKS):
        g, s = divmod(c, N_SUB)
        off = offset_of(s)
        if off < 0:
            extras.append("future")
            continue
        bias = bias_ref[0, off, g] if off <= 1 else None
        extras.append((bias, mask_of(c) if off >= 1 else None))
    return extras


def _attend(r, tile, vt_rows, bias_ref, mask_row, mask_wide):
    r.m[...] = jnp.full(r.m.shape, NEG, F32)
    r.acc[...] = jnp.zeros(r.acc.shape, F32)
    base = N_SUB * tile
    far = 2

    def mask_of(blk):
        return (lambda c: None) if mask_row is None else (lambda c: (lambda: mask_row(blk, c)))

    def generic(blk):
        return _visit_extras(bias_ref, lambda s: far, mask_of(blk))

    def block_of(n):
        return jnp.where(n < N_SUB, base + n, jnp.where(n == N_SUB, base - 1, n - N_SUB - 1))

    visits = {}

    def stage(n, blk, extras):
        visits[n] = (blk, extras)
        _stage_a(r, n % 2, blk, extras)
        if n >= 1:
            _stage_b(r, (n - 1) % 2, visits[n - 1][1])
        if n >= 2:
            _stage_c(r, n % 2, visits[n - 2][0], vt_rows, visits[n - 2][1])

    for d in range(N_SUB):
        stage(d, base + d, _visit_extras(bias_ref, lambda s: s - d, mask_of(base + d)))

    @pl.when(tile == 0)
    def _():
        _stage_b(r, 1, visits[N_SUB - 1][1])
        _stage_c(r, 0, visits[N_SUB - 2][0], vt_rows, visits[N_SUB - 2][1])
        _stage_c(r, 1, visits[N_SUB - 1][0], vt_rows, visits[N_SUB - 1][1])

    @pl.when(tile > 0)
    def _():
        stage(N_SUB, base - 1,
              _visit_extras(bias_ref, lambda s: 1 if s == 0 else far, mask_of(base - 1)))
        stage(N_SUB + 1, 0, generic(0))

        def body(t, carry):
            n = N_SUB + 2 + 2 * t
            first = 1 + 2 * t
            _stage_a_wide(r, 0, first, mask_wide)
            _stage_b_wide(r, 1, first - 1, mask_wide)
            _stage_c_wide(r, 0, block_of(n - 2), vt_rows)
            _stage_a_wide(r, 1, first + 1, mask_wide)
            _stage_b_wide(r, 0, first, mask_wide)
            _stage_c_wide(r, 1, first - 1, vt_rows)
            return carry

        lax.fori_loop(0, base // 2 - 1, body, 0)

        n_visits = base + N_SUB
        _stage_b(r, 1, generic(base - 2))
        _stage_c(r, 0, block_of(n_visits - 2), vt_rows, generic(0))
        _stage_c(r, 1, block_of(n_visits - 1), vt_rows, generic(0))


def _attn_scratch(acc_rows):
    return [
        pltpu.VMEM((PAIR, COLS), BF16),
        pltpu.VMEM((T, COLS_F32_PITCH), F32),
        pltpu.VMEM((T, COLS_F32_PITCH), F32),
        pltpu.VMEM((1, COLS), F32),
        pltpu.VMEM((1, COLS), F32),
        pltpu.VMEM((T, COLS), BF16),
        pltpu.VMEM((T, COLS), BF16),
        pltpu.VMEM((1, COLS), F32),
        pltpu.VMEM((1, COLS), F32),
        pltpu.VMEM((1, COLS), F32),
        pltpu.VMEM((acc_rows, COLS_F32_PITCH), F32),
    ]


def _make_refs(k_ref, vt_ref, scratch):
    q2, s0, s1, mb0, mb1, p0, p1, al0, al1, m, acc = scratch
    return _AttnRefs(k_ref, vt_ref, q2, (s0, s1), (mb0, mb1), (p0, p1), (al0, al1), m, acc)


def _attn_in_specs(nb, vt_rows):
    return [
        pl.BlockSpec((1, 1, N_QT, PAIR, TQS), lambda b, p, i: (b, p, i, 0, 0)),
        pl.BlockSpec((1, 1, nb, T, PAIR), lambda b, p, i: (b, p, 0, 0, 0)),
        pl.BlockSpec((1, 1, nb, vt_rows, T), lambda b, p, i: (b, p, 0, 0, 0)),
        pl.BlockSpec((1, 2, 2, T, T), lambda b, p, i: (p, 0, 0, 0, 0)),
    ]


def _diff_attn_kernel(qt_ref, k_ref, vt_ref, bias_ref, lam_ref, subln_ref, o_ref, *scratch):
    r = _make_refs(k_ref, vt_ref, scratch)
    _fill_q2(r.q2, qt_ref)
    _attend(r, pl.program_id(2), [slice(0, VT_ROWS_A)] * 2, bias_ref, None, None)

    lp = lam_ref[...]
    lam = (jnp.exp(jnp.sum(lp[0:1] * lp[1:2], axis=1, keepdims=True))
           - jnp.exp(jnp.sum(lp[2:3] * lp[3:4], axis=1, keepdims=True)) + LAM_INIT_A)
    o_all = r.acc[0:DV_A, 0:COLS] * (1.0 / r.acc[DV_A:DV_A + 1, 0:COLS])
    o = o_all[:, 0:TQ] - lam * o_all[:, TQ:COLS]
    o = o * lax.rsqrt(jnp.mean(o * o, axis=0, keepdims=True) + LN_EPS)
    o = o * subln_ref[...] * (1.0 - LAM_INIT_A)
    o_ref[0] = o.T.astype(BF16)


def _diff_attn(qt, k, vt, bias, lam_p, subln_col):
    bsz, _, nqs, _, _ = qt.shape
    nb = k.shape[2]
    return pl.pallas_call(
        _diff_attn_kernel,
        grid=(bsz, N_PAIRS, nqs // N_QT),
        in_specs=_attn_in_specs(nb, VT_ROWS_A) + [
            pl.BlockSpec((4, HEAD_DIM), lambda b, p, i: (0, 0)),
            pl.BlockSpec((PAIR, 1), lambda b, p, i: (0, 0)),
        ],
        out_specs=pl.BlockSpec((1, TQ, PAIR), lambda b, p, i: (b, i, p)),
        out_shape=jax.ShapeDtypeStruct((bsz, nqs * TQS, D_MODEL), BF16),
        scratch_shapes=_attn_scratch(VT_ROWS_A),
        compiler_params=_params(3),
        name="diff_attn",
    )(qt, k, vt, bias, lam_p, subln_col)


def _moba_attn_kernel(qt_ref, k_ref, vt_ref, bias_ref, mask_ref, o_ref, *scratch):
    r = _make_refs(k_ref, vt_ref, scratch)
    _fill_q2(r.q2, qt_ref)

    def mask_row(blk, c):
        g, s = divmod(c, N_SUB)
        t, s_in = divmod(s, TQS // T)
        col0 = g * TQS + s_in * T
        return mask_ref[0, 0, t, pl.ds(blk, 1), col0:col0 + T]

    def mask_wide(blk, u):
        g, t = divmod(u, N_QT)
        return mask_ref[0, 0, t, pl.ds(blk, 1), g * TQS:(g + 1) * TQS]

    rows = [slice(0, VT_ROWS_B), slice(VT_ROWS_B, 2 * VT_ROWS_B)]
    _attend(r, pl.program_id(2), rows, bias_ref, mask_row, mask_wide)

    o_all = r.acc[0:HEAD_DIM, 0:COLS] * (1.0 / r.acc[HEAD_DIM:HEAD_DIM + 1, 0:COLS])
    o = jnp.concatenate([o_all[:, 0:TQ], o_all[:, TQ:COLS]], axis=0)
    o_ref[0] = o.T.astype(BF16)


def _moba_attn(qt, k, vt, bias, mask):
    bsz, _, nqs, _, _ = qt.shape
    nb = k.shape[2]
    return pl.pallas_call(
        _moba_attn_kernel,
        grid=(bsz, N_PAIRS, nqs // N_QT),
        in_specs=_attn_in_specs(nb, 2 * VT_ROWS_B) + [
            pl.BlockSpec((1, 1, N_QT, nb, 2 * TQS), lambda b, p, i: (b, p, i, 0, 0)),
        ],
        out_specs=pl.BlockSpec((1, TQ, PAIR), lambda b, p, i: (b, i, p)),
        out_shape=jax.ShapeDtypeStruct((bsz, nqs * TQS, D_MODEL), BF16),
        scratch_shapes=_attn_scratch(VT_ROWS_B),
        compiler_params=_params(3),
        name="moba_attn",
    )(qt, k, vt, bias, mask)


def _post_attn_kernel(o_ref, h_ref, wo_ref, g1_ref, b1_ref, wup_ref, wdn_ref, g2_ref, b2_ref,
                      out_ref):
    y = ALPHA * h_ref[...] + jnp.dot(o_ref[...], wo_ref[...], preferred_element_type=F32)
    h = _layer_norm(y, g1_ref[...], b1_ref[...])
    hb = h.astype(BF16)
    y = ALPHA * h
    for c in range(D_FF // D_MODEL):
        cols = slice(c * D_MODEL, (c + 1) * D_MODEL)
        u = jnp.dot(hb, wup_ref[:, cols], preferred_element_type=F32)
        u = jnp.square(jnp.maximum(u, 0.0)).astype(BF16)
        y = y + jnp.dot(u, wdn_ref[cols, :], preferred_element_type=F32)
    out_ref[...] = _layer_norm(y, g2_ref[...], b2_ref[...])


def _post_attn(o, h, wo, g1, b1, wup, wdn, g2, b2):
    m = h.shape[0]
    tile = pl.BlockSpec((TM, D_MODEL), lambda i: (i, 0))
    return pl.pallas_call(
        _post_attn_kernel,
        grid=(m // TM,),
        in_specs=[
            tile,
            tile,
            _weight_spec((D_MODEL, D_MODEL)),
            _weight_spec((1, D_MODEL)),
            _weight_spec((1, D_MODEL)),
            _weight_spec((D_MODEL, D_FF)),
            _weight_spec((D_FF, D_MODEL)),
            _weight_spec((1, D_MODEL)),
            _weight_spec((1, D_MODEL)),
        ],
        out_specs=tile,
        out_shape=jax.ShapeDtypeStruct((m, D_MODEL), F32),
        compiler_params=_params(1),
        name="post_attn",
    )(o, h, wo, g1, b1, wup, wdn, g2, b2)


def kernel(x, w_in_a, lam_a, subln_a, w_out_a, w_kv_shared, w_q_b, w_out_b, rel_bias,
           ln1_g, ln1_b, ln2_g, ln2_b, w_up, w_down):
    bsz, seq, d = x.shape
    assert d == D_MODEL and seq % TM == 0 and TM % T == 0
    assert w_in_a.shape[0] == 1 and w_q_b.shape[0] == 1 and w_up.shape[0] == DEPTH
    m = bsz * seq
    row = lambda v: v.reshape(1, D_MODEL)

    bias = _bias_tiles(rel_bias)

    w_in = w_in_a[0]
    wqt = w_in[:, 0:D_MODEL].T.astype(BF16)
    wk = w_in[:, D_MODEL:2 * D_MODEL].astype(BF16)
    wvt = w_in[:, 2 * D_MODEL:].T.astype(BF16)
    qt, k, vt = _proj_a(x, wqt, wk, wvt)
    o = _diff_attn(qt, k, vt, bias, lam_a[0], subln_a[0].reshape(PAIR, 1))
    h = _post_attn(o.reshape(m, d), x.reshape(m, d), w_out_a[0].astype(BF16),
                   row(ln1_g[0]), row(ln1_b[0]), w_up[0].astype(BF16), w_down[0].astype(BF16),
                   row(ln2_g[0]), row(ln2_b[0]))

    h3 = h.reshape(bsz, seq, d)
    k2, vt2, kmean = _proj_kv(h3, w_kv_shared[:, 0:D_MODEL].astype(BF16),
                              w_kv_shared[:, D_MODEL:].T.astype(BF16))
    qt2, mask = _proj_q_gate(h3, w_q_b[0].T.astype(BF16), kmean)
    o = _moba_attn(qt2, k2, vt2, bias, mask)
    h = _post_attn(o.reshape(m, d), h, w_out_b[0].astype(BF16), row(ln1_g[1]), row(ln1_b[1]),
                   w_up[1].astype(BF16), w_down[1].astype(BF16), row(ln2_g[1]), row(ln2_b[1]))
    return h.reshape(bsz, seq, d)
```
